```python
import math
import jax
import jax.numpy as jnp
from jax import lax
import numpy as np

D_MODEL = 4096
BATCH = 2
SEQ = 4096
DEPTH = 1
DEC_BATCH = 8
DEC_SEQ = 64
PAST_LEN = 2048

CHUNK = 64
N_HEADS = 16
HEAD_DIM = 128
ATTN_WIDTH = N_HEADS * HEAD_DIM
SSM_WIDTH = 2048
SSM_GROUP = 16
SSM_GROUPS = SSM_WIDTH // SSM_GROUP
SSM_STATE = 64
N_EXPERTS = 32
TOP_K = 4
D_EXPERT = 4096
SWIGLU_LIMIT = 7.0
SWIGLU_ALPHA = 1.702
Q_BLOCK = 128
MOE_BLOCK = 128
NORM_EPS = 1e-6
DT_MIN = 1e-3
DT_MAX = 1e-1
SB_SCALE = HEAD_DIM ** -0.5
IN_WIDTH = 3 * ATTN_WIDTH + SSM_WIDTH + 2 * D_MODEL
IN_SPLITS = [ATTN_WIDTH, 2 * ATTN_WIDTH, 3 * ATTN_WIDTH, 3 * ATTN_WIDTH + SSM_WIDTH,
             3 * ATTN_WIDTH + SSM_WIDTH + D_MODEL]

kernel_name = "stickbreak_s5_moe_adaln_stream_step"

F32 = jnp.float32


def _rms_norm(x, g):
    xf = x.astype(F32)
    xf = xf * lax.rsqrt(jnp.mean(xf * xf, axis=-1, keepdims=True) + NORM_EPS)
    return xf * g.astype(F32)


def _sb_block(q, q_pos, k, v, k_pos):
    z = jnp.einsum('bqhd,bshd->bhqs', q.astype(F32), k.astype(F32)) * SB_SCALE
    mask = k_pos[None, :] < q_pos[:, None]
    log_keep = jnp.where(mask, jax.nn.log_sigmoid(-z), 0.0)
    later = lax.cumsum(log_keep, axis=3, reverse=True) - log_keep
    w = jnp.where(mask, jnp.exp(jax.nn.log_sigmoid(z) + later), 0.0)
    return jnp.einsum('bhqs,bshd->bqhd', w, v.astype(F32))


def _sb_prompt(q, k, v):
    bsz, T = q.shape[0], q.shape[1]
    nb = T // Q_BLOCK
    qb = q.reshape(bsz, nb, Q_BLOCK, N_HEADS, HEAD_DIM).transpose(1, 0, 2, 3, 4)
    q_pos = jnp.arange(T, dtype=jnp.int32).reshape(nb, Q_BLOCK)
    k_pos = jnp.arange(T, dtype=jnp.int32)
    ob = lax.map(lambda a: _sb_block(a[0], a[1], k, v, k_pos), (qb, q_pos))
    return ob.transpose(1, 0, 2, 3, 4).reshape(bsz, T, ATTN_WIDTH)


def _sb_sample(q, k, v, past_k, past_v):
    bsz, T = q.shape[0], q.shape[1]
    past = past_k.shape[1]
    k_all = jnp.concatenate([past_k.astype(k.dtype), k], axis=1)
    v_all = jnp.concatenate([past_v.astype(v.dtype), v], axis=1)
    q_pos = past + jnp.arange(T, dtype=jnp.int32)
    k_pos = jnp.arange(past + T, dtype=jnp.int32)
    return _sb_block(q, q_pos, k_all, v_all, k_pos).reshape(bsz, T, ATTN_WIDTH)


def _cmul_combine(e1, e2):
    a1r, a1i, b1r, b1i = e1
    a2r, a2i, b2r, b2i = e2
    return (a2r * a1r - a2i * a1i, a2r * a1i + a2i * a1r,
            a2r * b1r - a2i * b1i + b2r, a2r * b1i + a2i * b1r + b2i)


def _s5(u, x0_re, x0_im, a_re, a_im, log_dt, b_re, b_im, c_re, c_im, d_skip):
    bsz, T = u.shape[0], u.shape[1]
    uf = u.astype(F32).reshape(bsz, T, SSM_GROUPS, SSM_GROUP)
    lam_re = a_re.astype(F32)
    lam_im = a_im.astype(F32)
    dt = jnp.exp(log_dt.astype(F32))[:, None]
    mag = jnp.exp(lam_re * dt)
    ab_re = mag * jnp.cos(lam_im * dt)
    ab_im = mag * jnp.sin(lam_im * dt)
    den = lam_re * lam_re + lam_im * lam_im
    nr = ab_re - 1.0
    co_re = (nr * lam_re + ab_im * lam_im) / den
    co_im = (ab_im * lam_re - nr * lam_im) / den
    br = b_re.astype(F32)
    bi = b_im.astype(F32)
    bb_re = co_re[..., None] * br - co_im[..., None] * bi
    bb_im = co_re[..., None] * bi + co_im[..., None] * br
    bu_re = jnp.einsum('btgp,gnp->btgn', uf, bb_re)
    bu_im = jnp.einsum('btgp,gnp->btgn', uf, bb_im)
    if x0_re is not None:
        x0r = x0_re.astype(F32)
        x0i = x0_im.astype(F32)
        bu_re = bu_re.at[:, 0].add(ab_re * x0r - ab_im * x0i)
        bu_im = bu_im.at[:, 0].add(ab_re * x0i + ab_im * x0r)
    a_r = jnp.broadcast_to(ab_re, (1, T) + ab_re.shape)
    a_i = jnp.broadcast_to(ab_im, (1, T) + ab_im.shape)
    _, _, xr, xi = lax.associative_scan(_cmul_combine, (a_r, a_i, bu_re, bu_im), axis=1)
    y = (jnp.einsum('btgn,gpn->btgp', xr, c_re.astype(F32))
         - jnp.einsum('btgn,gpn->btgp', xi, c_im.astype(F32)))
    y = y + d_skip.astype(F32).reshape(SSM_GROUPS, SSM_GROUP) * uf
    return y.reshape(bsz, T, SSM_WIDTH), xr[:, -1], xi[:, -1]


def _moe(h, l, w_router, b_router, w_gate_up, b_gate_up, w_down, b_down):
    n_tok, dm = h.shape
    logits = h.astype(F32) @ w_router[l].astype(F32) + b_router[l].astype(F32)
    top_val, top_idx = lax.top_k(logits, TOP_K)
    probs = jax.nn.softmax(top_val, axis=-1)
    n_slot = n_tok * TOP_K
    flat_e = top_idx.reshape(n_slot).astype(jnp.int32)
    flat_tok = jnp.repeat(jnp.arange(n_tok, dtype=jnp.int32), TOP_K)
    order = jnp.argsort(flat_e)
    s_e = flat_e[order]
    s_tok = flat_tok[order]
    s_w = probs.reshape(n_slot)[order]
    counts = jnp.bincount(flat_e, length=N_EXPERTS).astype(jnp.int32)
    starts = jnp.cumsum(counts) - counts
    p_counts = (counts + MOE_BLOCK - 1) // MOE_BLOCK * MOE_BLOCK
    p_ends = jnp.cumsum(p_counts)
    p_starts = p_ends - p_counts
    dest = p_starts[s_e] + jnp.arange(n_slot, dtype=jnp.int32) - starts[s_e]
    n_blk = -(-n_slot // MOE_BLOCK) + N_EXPERTS
    rows = jnp.full((n_blk * MOE_BLOCK,), n_tok, jnp.int32).at[dest].set(s_tok)
    blk_start = jnp.arange(n_blk, dtype=jnp.int32) * MOE_BLOCK
    blk_e = jnp.minimum(jnp.searchsorted(p_ends, blk_start, side='right'), N_EXPERTS - 1).astype(jnp.int32)
    h_pad = jnp.concatenate([h, jnp.zeros((1, dm), h.dtype)], axis=0)
    xb = h_pad[rows].reshape(n_blk, MOE_BLOCK, dm)

    def expert_block(args):
        xblk, e = args
        gu = (xblk @ w_gate_up[l, e] + b_gate_up[l, e]).astype(F32)
        gate = jnp.minimum(gu[:, :D_EXPERT], SWIGLU_LIMIT)
        up = jnp.clip(gu[:, D_EXPERT:], -SWIGLU_LIMIT, SWIGLU_LIMIT)
        act = ((up + 1.0) * gate * jax.nn.sigmoid(SWIGLU_ALPHA * gate)).astype(h.dtype)
        return act @ w_down[l, e] + b_down[l, e]

    yb = lax.map(expert_block, (xb, blk_e)).reshape(n_blk * MOE_BLOCK, dm)
    y_slot = yb[dest] * s_w[:, None].astype(yb.dtype)
    return jax.ops.segment_sum(y_slot, s_tok, num_segments=n_tok)


def _layer(x, c, past_k, past_v, ssm_re0, ssm_im0, p, l):
    dt = x.dtype
    bsz, T, _ = x.shape
    mod = jax.nn.silu(c.astype(F32)) @ p['w_ada'][l].astype(F32) + p['b_ada'][l].astype(F32)
    sh_m, sc_m, gt_m, sh_f, sc_f, gt_f = jnp.split(mod[:, None, :], 6, axis=-1)
    h = (_rms_norm(x, p['g_norm_mix'][l]) * (1.0 + sc_m) + sh_m).astype(dt)
    proj = h @ p['w_in'][l] + p['b_in'][l]
    q, k, v, u, ga, gb = jnp.split(proj, IN_SPLITS, axis=-1)
    q = _rms_norm(q.reshape(bsz, T, N_HEADS, HEAD_DIM), p['g_q'][l]).astype(dt)
    k = _rms_norm(k.reshape(bsz, T, N_HEADS, HEAD_DIM), p['g_k'][l]).astype(dt)
    v = v.reshape(bsz, T, N_HEADS, HEAD_DIM)
    if past_k is None:
        attn = _sb_prompt(q, k, v)
    else:
        attn = _sb_sample(q, k, v, past_k, past_v)
    h_a = attn.astype(dt) @ p['w_attn_up'][l]
    y, s_re, s_im = _s5(u, ssm_re0, ssm_im0, p['ssm_a_re'][l], p['ssm_a_im'][l], p['ssm_log_dt'][l],
                        p['ssm_b_re'][l], p['ssm_b_im'][l], p['ssm_c_re'][l], p['ssm_c_im'][l], p['ssm_d'][l])
    yg = jax.nn.gelu(y, approximate=False).astype(dt)
    val, gate = jnp.split(yg @ p['w_glu'][l], 2, axis=-1)
    h_b = val * jax.nn.sigmoid(gate)
    merged = jax.nn.sigmoid(ga) * h_a + jax.nn.sigmoid(gb) * h_b
    x = x + gt_m.astype(dt) * (merged @ p['w_out'][l])
    h2 = (_rms_norm(x, p['g_norm_ffn'][l]) * (1.0 + sc_f) + sh_f).astype(dt)
    ffn = _moe(h2.reshape(bsz * T, D_MODEL), l, p['w_router'], p['b_router'], p['w_gate_up'],
               p['b_gate_up'], p['w_down'], p['b_down']).reshape(bsz, T, D_MODEL)
    x = x + gt_f.astype(dt) * ffn
    return x, k, v, s_re.astype(dt), s_im.astype(dt)


def setup_inputs(seed: int = 0) -> dict:
    key = jax.random.key(seed)
    ks = jax.random.split(key, 40)

    def nrm(k, shape, scale):
        return jax.random.normal(k, shape, F32) * scale

    L = DEPTH
    n_idx = jnp.arange(SSM_STATE, dtype=F32)
    return {
        'x_prompt': nrm(ks[0], (BATCH, SEQ, D_MODEL), 1.0),
        'x_sample': nrm(ks[1], (DEC_BATCH, DEC_SEQ, D_MODEL), 1.0),
        'cache_k': nrm(ks[2], (L, DEC_BATCH, PAST_LEN, N_HEADS, HEAD_DIM), 1.0),
        'cache_v': nrm(ks[3], (L, DEC_BATCH, PAST_LEN, N_HEADS, HEAD_DIM), 1.0),
        'state_ssm_re': nrm(ks[4], (L, DEC_BATCH, SSM_GROUPS, SSM_STATE), 0.1),
        'state_ssm_im': nrm(ks[5], (L, DEC_BATCH, SSM_GROUPS, SSM_STATE), 0.1),
        'c_prompt': nrm(ks[6], (BATCH, D_MODEL), 1.0),
        'c_sample': nrm(ks[7], (DEC_BATCH, D_MODEL), 1.0),
        'w_ada': nrm(ks[8], (L, D_MODEL, 6 * D_MODEL), 0.005),
        'b_ada': nrm(ks[9], (L, 6 * D_MODEL), 0.02),
        'g_norm_mix': 1.0 + nrm(ks[10], (L, D_MODEL), 0.02),
        'g_norm_ffn': 1.0 + nrm(ks[11], (L, D_MODEL), 0.02),
        'w_in': nrm(ks[12], (L, D_MODEL, IN_WIDTH), D_MODEL ** -0.5),
        'b_in': nrm(ks[13], (L, IN_WIDTH), 0.02),
        'g_q': 1.0 + nrm(ks[14], (L, HEAD_DIM), 0.02),
        'g_k': 1.0 + nrm(ks[15], (L, HEAD_DIM), 0.02),
        'w_attn_up': nrm(ks[16], (L, ATTN_WIDTH, D_MODEL), ATTN_WIDTH ** -0.5),
        'ssm_a_re': -0.5 + nrm(ks[17], (L, SSM_GROUPS, SSM_STATE), 0.01),
        'ssm_a_im': math.pi * n_idx + nrm(ks[18], (L, SSM_GROUPS, SSM_STATE), 0.01),
        'ssm_log_dt': jax.random.uniform(ks[19], (L, SSM_GROUPS), F32, math.log(DT_MIN), math.log(DT_MAX)),
        'ssm_b_re': nrm(ks[20], (L, SSM_GROUPS, SSM_STATE, SSM_GROUP), (2 * SSM_GROUP) ** -0.5),
        'ssm_b_im': nrm(ks[21], (L, SSM_GROUPS, SSM_STATE, SSM_GROUP), (2 * SSM_GROUP) ** -0.5),
        'ssm_c_re': nrm(ks[22], (L, SSM_GROUPS, SSM_GROUP, SSM_STATE), SSM_STATE ** -0.5),
        'ssm_c_im': nrm(ks[23], (L, SSM_GROUPS, SSM_GROUP, SSM_STATE), SSM_STATE ** -0.5),
        'ssm_d': nrm(ks[24], (L, SSM_WIDTH), 1.0),
        'w_glu': nrm(ks[25], (L, SSM_WIDTH, 2 * D_MODEL), SSM_WIDTH ** -0.5),
        'w_out': nrm(ks[26], (L, D_MODEL, D_MODEL), D_MODEL ** -0.5),
        'w_router': nrm(ks[27], (L, D_MODEL, N_EXPERTS), D_MODEL ** -0.5),
        'b_router': nrm(ks[28], (L, N_EXPERTS), 0.01),
        'w_gate_up': nrm(ks[29], (L, N_EXPERTS, D_MODEL, 2 * D_EXPERT), D_MODEL ** -0.5),
        'b_gate_up': nrm(ks[30], (L, N_EXPERTS, 2 * D_EXPERT), 0.01),
        'w_down': nrm(ks[31], (L, N_EXPERTS, D_EXPERT, D_MODEL), D_EXPERT ** -0.5),
        'b_down': nrm(ks[32], (L, N_EXPERTS, D_MODEL), 0.01),
    }


def reference(x_prompt, x_sample, cache_k, cache_v, state_ssm_re, state_ssm_im, c_prompt, c_sample,
              w_ada, b_ada, g_norm_mix, g_norm_ffn, w_in, b_in, g_q, g_k, w_attn_up,
              ssm_a_re, ssm_a_im, ssm_log_dt, ssm_b_re, ssm_b_im, ssm_c_re, ssm_c_im, ssm_d,
              w_glu, w_out, w_router, b_router, w_gate_up, b_gate_up, w_down, b_down):
    p = dict(w_ada=w_ada, b_ada=b_ada, g_norm_mix=g_norm_mix, g_norm_ffn=g_norm_ffn, w_in=w_in, b_in=b_in,
             g_q=g_q, g_k=g_k, w_attn_up=w_attn_up, ssm_a_re=ssm_a_re, ssm_a_im=ssm_a_im,
             ssm_log_dt=ssm_log_dt, ssm_b_re=ssm_b_re, ssm_b_im=ssm_b_im, ssm_c_re=ssm_c_re,
             ssm_c_im=ssm_c_im, ssm_d=ssm_d, w_glu=w_glu, w_out=w_out, w_router=w_router,
             b_router=b_router, w_gate_up=w_gate_up, b_gate_up=b_gate_up, w_down=w_down, b_down=b_down)
    xp = x_prompt
    xs = x_sample
    kp_l, vp_l, srp_l, sip_l = [], [], [], []
    ks_l, vs_l, srs_l, sis_l = [], [], [], []
    for l in range(DEPTH):
        xp, kp, vp, srp, sip = _layer(xp, c_prompt, None, None, None, None, p, l)
        xs, kss, vss, srs, sis = _layer(xs, c_sample, cache_k[l], cache_v[l],
                                        state_ssm_re[l], state_ssm_im[l], p, l)
        kp_l.append(kp)
        vp_l.append(vp)
        srp_l.append(srp)
        sip_l.append(sip)
        ks_l.append(kss)
        vs_l.append(vss)
        srs_l.append(srs)
        sis_l.append(sis)
    k_prompt = jnp.stack(kp_l)
    v_prompt = jnp.stack(vp_l)
    ssm_re_prompt = jnp.stack(srp_l)
    ssm_im_prompt = jnp.stack(sip_l)
    k_sample = jnp.stack(ks_l)
    v_sample = jnp.stack(vs_l)
    ssm_re_sample = jnp.stack(srs_l)
    ssm_im_sample = jnp.stack(sis_l)
    return (xp, xs, k_prompt, v_prompt, ssm_re_prompt, ssm_im_prompt,
            k_sample, v_sample, ssm_re_sample, ssm_im_sample)
```

```python
import functools
import math

import jax
import jax.numpy as jnp
from jax import lax
from jax.experimental import pallas as pl
from jax.experimental.pallas import tpu as pltpu

F32 = jnp.float32
BF16 = jnp.bfloat16

NORM_EPS = 1e-6
TOP_K = 4
SWIGLU_LIMIT = 7.0
SWIGLU_ALPHA = 1.702
ROW_BLOCK = 64
SSM_CHUNK = 16
LANES = 128
VMEM_LIMIT = 56 * 1024 * 1024


def _cp(sem, vmem=VMEM_LIMIT):
    return pltpu.CompilerParams(dimension_semantics=sem, vmem_limit_bytes=vmem)


def _pick(n, pref):
    t = pref
    while t > 1 and n % t:
        t //= 2
    return t


def _sigmoid(x):
    return 1.0 / (1.0 + jnp.exp(-x))


def _dot(a, b):
    return jnp.dot(a, b, preferred_element_type=F32)


def _split(a):
    hi = a.astype(BF16)
    lo = (a - hi.astype(F32)).astype(BF16)
    return hi, lo


def _dot3(a, b):
    ah, al = _split(a)
    bh, bl = _split(b)
    return _dot(ah, bh) + (_dot(al, bh) + _dot(ah, bl))


class _Geom:
    def __init__(self, B, T, DB, DT):
        self.B, self.T, self.DB, self.DT = B, T, DB, DT
        self.n_p, self.n_s = B * T, DB * DT
        self.n_tok = self.n_p + self.n_s
        assert T % ROW_BLOCK == 0 and DT % ROW_BLOCK == 0
        self.tm = 512
        while self.n_p % self.tm or self.n_s % self.tm:
            self.tm //= 2
        assert self.tm >= ROW_BLOCK
        self.sub = self.tm // ROW_BLOCK
        self.nb = B + DB
        self.nbp = -(-self.nb // 8) * 8

    def bidx(self, blk):
        pb = self.n_p // ROW_BLOCK
        return jnp.where(blk < pb, blk // (self.T // ROW_BLOCK),
                         self.B + (blk - pb) // (self.DT // ROW_BLOCK))


def _ada_body(c_ref, w_ref, b_ref, o_ref):
    c = c_ref[...]
    s = c * _sigmoid(c)
    o_ref[...] = _dot(s.astype(BF16), w_ref[...].astype(BF16)) + b_ref[...]


def _ada(c_pad, w, b, l):
    nbp, d = c_pad.shape
    n = w.shape[1]
    tn = _pick(n, 1024)
    return pl.pallas_call(
        _ada_body,
        grid=(n // tn,),
        in_specs=[pl.BlockSpec((nbp, d), lambda j: (0, 0)),
                  pl.BlockSpec((d, tn), lambda j: (l, j)),
                  pl.BlockSpec((1, tn), lambda j: (l, j))],
        out_specs=pl.BlockSpec((nbp, tn), lambda j: (0, j)),
        out_shape=jax.ShapeDtypeStruct((nbp, n), F32),
        compiler_params=_cp(("arbitrary",)),
        name="ada_mod",
    )(c_pad, w, b)


def _norm_rows(x, g, sc, sh):
    ms = jnp.mean(x * x, axis=-1, keepdims=True)
    return (x * lax.rsqrt(ms + NORM_EPS)) * g * (1.0 + sc) + sh


def _norm_mod_body(x_ref, g_ref, mod_ref, o_ref, *, geom, i_sh, i_sc):
    i = pl.program_id(0)
    for r in range(geom.sub):
        b = geom.bidx(i * geom.sub + r)
        rows = pl.ds(r * ROW_BLOCK, ROW_BLOCK)
        h = _norm_rows(x_ref[rows, :], g_ref[...], mod_ref[b, pl.ds(i_sc, 1), :], mod_ref[b, pl.ds(i_sh, 1), :])
        o_ref[rows, :] = h.astype(o_ref.dtype)


def _norm_mod(x, g, mod3, geom, l, i_sh, i_sc):
    n, d = x.shape
    tm = geom.tm
    return pl.pallas_call(
        functools.partial(_norm_mod_body, geom=geom, i_sh=i_sh, i_sc=i_sc),
        grid=(n // tm,),
        in_specs=[pl.BlockSpec((tm, d), lambda i: (i, 0)),
                  pl.BlockSpec((1, d), lambda i: (l, 0)),
                  pl.BlockSpec(mod3.shape, lambda i: (0, 0, 0))],
        out_specs=pl.BlockSpec((tm, d), lambda i: (i, 0)),
        out_shape=jax.ShapeDtypeStruct((n, d), BF16),
        compiler_params=_cp(("arbitrary",)),
        name="norm_mod",
    )(x, g, mod3)


def _head_norm(acc, g):
    outs = []
    for c in range(acc.shape[1] // LANES):
        a = acc[:, c * LANES:(c + 1) * LANES]
        ms = jnp.mean(a * a, axis=-1, keepdims=True)
        outs.append(a * lax.rsqrt(ms + NORM_EPS) * g)
    return outs


def _dense_body(*refs, mode, geom, n_extra, n_out, has_bias):
    x_ref, w_ref = refs[0], refs[1]
    pos = 2
    b_ref = None
    if has_bias:
        b_ref = refs[pos]
        pos += 1
    extra = refs[pos:pos + n_extra]
    outs = refs[pos + n_extra:pos + n_extra + n_out]
    wb = refs[pos + n_extra + n_out]
    i = pl.program_id(1)

    @pl.when(i == 0)
    def _():
        wb[...] = w_ref[...].astype(BF16)

    acc = _dot(x_ref[...], wb[...])
    if has_bias:
        acc = acc + b_ref[...]
    if mode == "q":
        for c, a in enumerate(_head_norm(acc, extra[0][...])):
            outs[0][:, c * LANES:(c + 1) * LANES] = a.astype(BF16)
    elif mode == "k":
        for c, a in enumerate(_head_norm(acc, extra[0][...])):
            outs[0][:, c * LANES:(c + 1) * LANES] = a
            outs[1][:, c * LANES:(c + 1) * LANES] = a.astype(BF16)
    elif mode == "v":
        outs[0][...] = acc
        outs[1][...] = acc.astype(BF16)
    elif mode == "plain":
        outs[0][...] = acc
    elif mode == "sig":
        outs[0][...] = _sigmoid(acc)
    elif mode == "mul":
        outs[0][...] = acc * extra[0][...]
    elif mode == "resid":
        xres, mod_ref = extra
        for r in range(geom.sub):
            b = geom.bidx(i * geom.sub + r)
            rows = pl.ds(r * ROW_BLOCK, ROW_BLOCK)
            gate = mod_ref[b, pl.ds(2, 1), :]
            outs[0][rows, :] = xres[rows, :] + gate * acc[r * ROW_BLOCK:(r + 1) * ROW_BLOCK, :]
    else:
        raise ValueError(mode)


def _dense(x, w, l, col0, ncols, *, mode, geom, bias=None, extra=(), extra_kinds=(), out_dtypes=(F32,),
           tn_pref=512, name="dense"):
    m, k = x.shape
    tm = geom.tm
    tn = _pick(math.gcd(ncols, col0) if col0 else ncols, tn_pref)
    cb = col0 // tn
    in_specs = [pl.BlockSpec((tm, k), lambda j, i: (i, 0)),
                pl.BlockSpec((k, tn), lambda j, i: (l, cb + j))]
    args = [x, w]
    if bias is not None:
        in_specs.append(pl.BlockSpec((1, tn), lambda j, i: (l, cb + j)))
        args.append(bias)
    for a, kind in zip(extra, extra_kinds):
        if kind == "tile":
            in_specs.append(pl.BlockSpec((tm, tn), lambda j, i: (i, j)))
        elif kind == "row":
            in_specs.append(pl.BlockSpec((1, a.shape[1]), lambda j, i: (l, 0)))
        elif kind == "mod":
            in_specs.append(pl.BlockSpec((a.shape[0], a.shape[1], tn), lambda j, i: (0, 0, j)))
        args.append(a)
    out_specs = [pl.BlockSpec((tm, tn), lambda j, i: (i, j)) for _ in out_dtypes]
    out_shape = [jax.ShapeDtypeStruct((m, ncols), dt) for dt in out_dtypes]
    res = pl.pallas_call(
        functools.partial(_dense_body, mode=mode, geom=geom, n_extra=len(extra), n_out=len(out_dtypes),
                          has_bias=bias is not None),
        grid=(ncols // tn, m // tm),
        in_specs=in_specs,
        out_specs=out_specs,
        out_shape=out_shape,
        scratch_shapes=[pltpu.VMEM((k, tn), BF16)],
        compiler_params=_cp(("arbitrary", "arbitrary")),
        name=name,
    )(*args)
    return res


def _glu_body(x_ref, wv_ref, wg_ref, ma_ref, sgb_ref, o_ref, wvb, wgb):
    @pl.when(pl.program_id(1) == 0)
    def _():
        wvb[...] = wv_ref[...].astype(BF16)
        wgb[...] = wg_ref[...].astype(BF16)

    x = x_ref[...]
    val = _dot(x, wvb[...])
    gate = _dot(x, wgb[...])
    o_ref[...] = (ma_ref[...] + sgb_ref[...] * (val * _sigmoid(gate))).astype(o_ref.dtype)


def _glu_merge(yg, w, l, ma, sgb, geom):
    m, k = yg.shape
    d = ma.shape[1]
    tm = geom.tm
    tn = _pick(d, 512)
    nj = d // tn
    return pl.pallas_call(
        _glu_body,
        grid=(nj, m // tm),
        in_specs=[pl.BlockSpec((tm, k), lambda j, i: (i, 0)),
                  pl.BlockSpec((k, tn), lambda j, i: (l, j)),
                  pl.BlockSpec((k, tn), lambda j, i: (l, nj + j)),
                  pl.BlockSpec((tm, tn), lambda j, i: (i, j)),
                  pl.BlockSpec((tm, tn), lambda j, i: (i, j))],
        out_specs=pl.BlockSpec((tm, tn), lambda j, i: (i, j)),
        out_shape=jax.ShapeDtypeStruct((m, d), BF16),
        scratch_shapes=[pltpu.VMEM((k, tn), BF16), pltpu.VMEM((k, tn), BF16)],
        compiler_params=_cp(("arbitrary", "arbitrary")),
        name="glu_merge",
    )(yg, w, w, ma, sgb)


def _softplus(z):
    return jnp.maximum(z, 0.0) + jnp.log(1.0 + jnp.exp(-jnp.abs(z)))


def _sb_tile(q, kt, vt, tri, mask, c, o, scale):
    z = lax.dot_general(q, kt, (((1,), (1,)), ((), ())), preferred_element_type=F32) * scale
    sp = _softplus(z)
    if mask is not None:
        sp = jnp.where(mask, sp, 0.0)
    hi, lo = _split(sp)
    later = _dot(hi, tri) + _dot(lo, tri)
    w = jnp.exp(z - sp - later - c)
    if mask is not None:
        w = jnp.where(mask, w, 0.0)
    o = o + _dot(w.astype(BF16), vt)
    c = c + (later[:, 0:1] + sp[:, 0:1])
    return c, o


def _suffix_matrix(tk):
    j = lax.broadcasted_iota(jnp.int32, (tk, tk), 0)
    s = lax.broadcasted_iota(jnp.int32, (tk, tk), 1)
    return (j > s).astype(BF16)


def _sb_prompt_body(q_ref, k_ref, v_ref, o_ref, *, tq, scale):
    qi = pl.program_id(2)
    q = q_ref[...]
    tri = _suffix_matrix(tq)
    row = lax.broadcasted_iota(jnp.int32, (tq, tq), 0)
    col = lax.broadcasted_iota(jnp.int32, (tq, tq), 1)
    diag = pl.ds(pl.multiple_of(qi * tq, tq), tq)
    c0 = jnp.zeros((tq, 1), F32)
    o0 = jnp.zeros((tq, q.shape[1]), F32)
    c, o = _sb_tile(q, k_ref[diag, :], v_ref[diag, :], tri, col < row, c0, o0, scale)

    def body(it, carry):
        ks = pl.ds(pl.multiple_of((qi - 1 - it) * tq, tq), tq)
        return _sb_tile(q, k_ref[ks, :], v_ref[ks, :], tri, None, carry[0], carry[1], scale)

    c, o = lax.fori_loop(0, qi, body, (c, o))
    o_ref[...] = o.astype(o_ref.dtype)


def _sb_prompt(q, kb, vb, B, T, H, hd):
    tq = _pick(T, 256)
    nq = T // tq
    return pl.pallas_call(
        functools.partial(_sb_prompt_body, tq=tq, scale=hd ** -0.5),
        grid=(B, H, nq),
        in_specs=[pl.BlockSpec((tq, hd), lambda b, h, i: (b * nq + i, h)),
                  pl.BlockSpec((T, hd), lambda b, h, i: (b, h)),
                  pl.BlockSpec((T, hd), lambda b, h, i: (b, h))],
        out_specs=pl.BlockSpec((tq, hd), lambda b, h, i: (b * nq + i, h)),
        out_shape=jax.ShapeDtypeStruct((B * T, H * hd), BF16),
        compiler_params=_cp(("arbitrary", "arbitrary", "arbitrary")),
        name="sb_prompt",
    )(q, kb, vb)


def _sb_sample_body(q_ref, k_ref, v_ref, ck_ref, cv_ref, o_ref, *, dt, tk, past, scale):
    q = q_ref[...]
    row = lax.broadcasted_iota(jnp.int32, (dt, dt), 0)
    col = lax.broadcasted_iota(jnp.int32, (dt, dt), 1)
    c0 = jnp.zeros((dt, 1), F32)
    o0 = jnp.zeros((dt, q.shape[1]), F32)
    c, o = _sb_tile(q, k_ref[...], v_ref[...], _suffix_matrix(dt), col < row, c0, o0, scale)
    tri = _suffix_matrix(tk)
    n_t = past // tk

    def body(it, carry):
        ks = pl.ds(pl.multiple_of((n_t - 1 - it) * tk, tk), tk)
        return _sb_tile(q, ck_ref[0, ks, :].astype(BF16), cv_ref[0, ks, :].astype(BF16), tri, None,
                        carry[0], carry[1], scale)

    c, o = lax.fori_loop(0, n_t, body, (c, o))
    o_ref[...] = o.astype(o_ref.dtype)


def _sb_sample(q, kb, vb, ck, cv, l, DB, DT, H, hd, n_p):
    past = ck.shape[2]
    tk = _pick(past, 256)
    rb0 = n_p // DT
    ck3 = ck.reshape(ck.shape[0] * DB, past, H * hd)
    cv3 = cv.reshape(cv.shape[0] * DB, past, H * hd)
    new = lambda b, h: (rb0 + b, h)
    return pl.pallas_call(
        functools.partial(_sb_sample_body, dt=DT, tk=tk, past=past, scale=hd ** -0.5),
        grid=(DB, H),
        in_specs=[pl.BlockSpec((DT, hd), new),
                  pl.BlockSpec((DT, hd), new),
                  pl.BlockSpec((DT, hd), new),
                  pl.BlockSpec((1, past, hd), lambda b, h: (l * DB + b, 0, h)),
                  pl.BlockSpec((1, past, hd), lambda b, h: (l * DB + b, 0, h))],
        out_specs=pl.BlockSpec((DT, hd), lambda b, h: (b, h)),
        out_shape=jax.ShapeDtypeStruct((DB * DT, H * hd), BF16),
        compiler_params=_cp(("arbitrary", "arbitrary")),
        name="sb_sample",
    )(q, kb, vb, ck3, cv3)


def _cexp(lr, li, t):
    mag = jnp.exp(lr * t)
    return mag * jnp.cos(li * t), mag * jnp.sin(li * t)


def _ssm_body(ut_ref, x0_ref, lidx_ref, lam_ref, lamc_ref, ldt_ref, bt_ref, ct_ref, d_ref,
              y_ref, fin_ref, e_scr, *, n_state, n_steps, last_rows):
    L = SSM_CHUNK
    N = n_state
    P = 16
    W = L * P
    dt = jnp.exp(ldt_ref[0])
    lr, li = lam_ref[0, 0:1, :], lam_ref[0, 1:2, :]
    lrc, lic = lamc_ref[0, :, 0:1], lamc_ref[0, :, 1:2]

    abr, abi = _cexp(lr, li, dt)
    den = lr * lr + li * li
    nr = abr - 1.0
    cor = (nr * lr + abi * li) / den
    coi = (abi * lr - nr * li) / den
    br, bi = bt_ref[0, 0], bt_ref[0, 1]
    bbr = cor * br - coi * bi
    bbi = cor * bi + coi * br
    jrow = (lax.broadcasted_iota(jnp.int32, (W, N), 0) // P).astype(F32)
    pr, pi = _cexp(lr, li, -jrow * dt)
    xr = bbr * pr - bbi * pi
    xi = bbr * pi + bbi * pr
    qr, qi = _cexp(lr, li, (float(L - 1) - jrow) * dt)
    wm = jnp.concatenate([bbr * qr - bbi * qi, bbr * qi + bbi * qr], axis=1)

    cr, ci = ct_ref[0, 0], ct_ref[0, 1]
    icol = (lax.broadcasted_iota(jnp.int32, (N, W), 1) // P).astype(F32)
    ar, ai = _cexp(lrc, lic, icol * dt)
    yr = cr * ar - ci * ai
    yi = cr * ai + ci * ar
    ar1, ai1 = _cexp(lrc, lic, (icol + 1.0) * dt)
    vm = jnp.concatenate([cr * ar1 - ci * ai1, -(cr * ai1 + ci * ar1)], axis=0)

    tm = _dot3(jnp.concatenate([xr, xi], axis=1), jnp.concatenate([yr, -yi], axis=0))
    rr = lax.broadcasted_iota(jnp.int32, (W, W), 0)
    cc = lax.broadcasted_iota(jnp.int32, (W, W), 1)
    tm = jnp.where(cc // P >= rr // P, tm, 0.0) + jnp.where(rr == cc, d_ref[0], 0.0)

    u = ut_ref[0]
    s_loc = _dot3(u, wm)

    def cmul(t, s):
        pr_, pi_ = _cexp(lr, li, t * dt)
        a_r = jnp.concatenate([pr_, pr_], axis=1)
        a_i = jnp.concatenate([-pi_, pi_], axis=1)
        return a_r * s + a_i * pltpu.roll(s, N, axis=1)

    lidx = lidx_ref[...]
    x0 = x0_ref[0]
    e = s_loc + cmul(float(L), x0)
    for kk in range(n_steps):
        sh = pltpu.roll(e, 1 << kk, axis=0)
        e = e + jnp.where(lidx >= (1 << kk), cmul(float(L * (1 << kk)), sh), 0.0)
    s_in = jnp.where(lidx >= 1, pltpu.roll(e, 1, axis=0), x0)
    y = _dot3(u, tm) + _dot3(s_in, vm)
    y_ref[0] = (0.5 * y * (1.0 + lax.erf(y * (2.0 ** -0.5)))).astype(y_ref.dtype)
    e_scr[...] = e
    for r, src in enumerate(last_rows):
        fin_ref[0, pl.ds(r, 1), :] = e_scr[pl.ds(src, 1), :]


def _ssm(ut, x0rows, lidx, lam, lamc, ldt, bt, ct, dtile, l, n_state, n_steps, last_rows):
    G, NC, W = ut.shape
    nb = len(last_rows)
    g_of = lambda g: (l * G + g, 0, 0)
    g4 = lambda g: (l * G + g, 0, 0, 0)
    return pl.pallas_call(
        functools.partial(_ssm_body, n_state=n_state, n_steps=n_steps, last_rows=last_rows),
        grid=(G,),
        in_specs=[pl.BlockSpec((1, NC, W), lambda g: (g, 0, 0)),
                  pl.BlockSpec((1, NC, 2 * n_state), lambda g: (g, 0, 0)),
                  pl.BlockSpec((NC, 1), lambda g: (0, 0)),
                  pl.BlockSpec((1, 2, n_state), g_of),
                  pl.BlockSpec((1, n_state, 2), g_of),
                  pl.BlockSpec((1, 1, 1), g_of),
                  pl.BlockSpec((1, 2, W, n_state), g4),
                  pl.BlockSpec((1, 2, n_state, W), g4),
                  pl.BlockSpec((1, 1, W), g_of)],
        out_specs=[pl.BlockSpec((1, NC, W), lambda g: (g, 0, 0)),
                   pl.BlockSpec((1, nb, 2 * n_state), lambda g: (g, 0, 0))],
        out_shape=[jax.ShapeDtypeStruct((G, NC, W), BF16),
                   jax.ShapeDtypeStruct((G, nb, 2 * n_state), F32)],
        scratch_shapes=[pltpu.VMEM((NC, 2 * n_state), F32)],
        compiler_params=_cp(("arbitrary",)),
        name="s5_group",
    )(ut, x0rows, lidx, lam, lamc, ldt, bt, ct, dtile)


def _norm_router_body(x_ref, g_ref, mod_ref, wr_ref, br_ref, h_ref, idx_ref, p_ref, *, geom, n_exp):
    i = pl.program_id(0)
    for r in range(geom.sub):
        b = geom.bidx(i * geom.sub + r)
        rows = pl.ds(r * ROW_BLOCK, ROW_BLOCK)
        h_ref[rows, :] = _norm_rows(x_ref[rows, :], g_ref[...], mod_ref[b, pl.ds(4, 1), :],
                                    mod_ref[b, pl.ds(3, 1), :])
    logits = _dot3(h_ref[...], wr_ref[...]) + br_ref[...]
    tm = logits.shape[0]
    lane = lax.broadcasted_iota(jnp.int32, (tm, n_exp), 1)
    wide = lax.broadcasted_iota(jnp.int32, (tm, LANES), 1)
    idx_out = jnp.zeros((tm, LANES), jnp.int32)
    vals = []
    for k in range(TOP_K):
        m = jnp.max(logits, axis=-1, keepdims=True)
        sel = jnp.min(jnp.where(logits == m, lane, n_exp), axis=-1, keepdims=True)
        idx_out = jnp.where(wide == k, sel, idx_out)
        vals.append(m)
        logits = jnp.where(lane == sel, -jnp.inf, logits)
    es = [jnp.exp(v - vals[0]) for v in vals]
    tot = es[0] + es[1] + es[2] + es[3]
    p_out = jnp.zeros((tm, LANES), F32)
    for k in range(TOP_K):
        p_out = jnp.where(wide == k, es[k] / tot, p_out)
    idx_ref[...] = idx_out
    p_ref[...] = p_out


def _norm_router(x, g, mod3, wr, br, geom, l):
    n, d = x.shape
    tm = geom.tm
    n_exp = wr.shape[1]
    return pl.pallas_call(
        functools.partial(_norm_router_body, geom=geom, n_exp=n_exp),
        grid=(n // tm,),
        in_specs=[pl.BlockSpec((tm, d), lambda i: (i, 0)),
                  pl.BlockSpec((1, d), lambda i: (l, 0)),
                  pl.BlockSpec(mod3.shape, lambda i: (0, 0, 0)),
                  pl.BlockSpec((d, n_exp), lambda i: (l, 0)),
                  pl.BlockSpec((1, n_exp), lambda i: (l, 0))],
        out_specs=[pl.BlockSpec((tm, d), lambda i: (i, 0)),
                   pl.BlockSpec((tm, LANES), lambda i: (i, 0)),
                   pl.BlockSpec((tm, LANES), lambda i: (i, 0))],
        out_shape=[jax.ShapeDtypeStruct((n, d), F32),
                   jax.ShapeDtypeStruct((n, LANES), jnp.int32),
                   jax.ShapeDtypeStruct((n, LANES), F32)],
        compiler_params=_cp(("arbitrary",)),
        name="norm_router",
    )(x, g, mod3, wr, br)


def _row_copy(src, dst, sem, s_row, d_slot):
    return pltpu.make_async_copy(src.at[pl.ds(s_row, 1), :], dst.at[pl.ds(d_slot, 1), :], sem)


def _gather_body(rows_ref, src_ref, o_ref, buf, sem, *, n_rows):
    base = pl.program_id(0) * n_rows

    def start(r, carry):
        _row_copy(src_ref, buf, sem, rows_ref[base + r], r).start()
        return carry

    def wait(r, carry):
        _row_copy(src_ref, buf, sem, 0, r).wait()
        return carry

    lax.fori_loop(0, n_rows, start, 0)
    lax.fori_loop(0, n_rows, wait, 0)
    o_ref[...] = buf[...].astype(o_ref.dtype)


def _moe_gather(rows, h2, tm):
    p_max = rows.shape[0]
    d = h2.shape[1]
    return pl.pallas_call(
        functools.partial(_gather_body, n_rows=tm),
        grid_spec=pltpu.PrefetchScalarGridSpec(
            num_scalar_prefetch=1,
            grid=(p_max // tm,),
            in_specs=[pl.BlockSpec(memory_space=pl.ANY)],
            out_specs=pl.BlockSpec((tm, d), lambda i, rows: (i, 0)),
            scratch_shapes=[pltpu.VMEM((tm, d), F32), pltpu.SemaphoreType.DMA(())]),
        out_shape=jax.ShapeDtypeStruct((p_max, d), BF16),
        compiler_params=_cp(("arbitrary",)),
        name="moe_gather",
    )(rows, h2)


def _expert_changed(blk_e, i):
    return (i == 0) | (blk_e[i] != blk_e[jnp.maximum(i - 1, 0)])


def _moe_gu_body(blk_e, row_blk, nv, x_ref, wg_ref, wu_ref, bg_ref, bu_ref, o_ref, wgb, wub):
    i = pl.program_id(1)

    @pl.when(_expert_changed(blk_e, i))
    def _():
        wgb[...] = wg_ref[0].astype(BF16)
        wub[...] = wu_ref[0].astype(BF16)

    @pl.when(i < nv[0])
    def _():
        x = x_ref[...]
        g = jnp.minimum(_dot(x, wgb[...]) + bg_ref[0], SWIGLU_LIMIT)
        u = jnp.clip(_dot(x, wub[...]) + bu_ref[0], -SWIGLU_LIMIT, SWIGLU_LIMIT)
        o_ref[...] = ((u + 1.0) * g * _sigmoid(SWIGLU_ALPHA * g)).astype(o_ref.dtype)

    @pl.when(i >= nv[0])
    def _():
        o_ref[...] = jnp.zeros(o_ref.shape, o_ref.dtype)


def _moe_gate_up(blk_e, row_blk, nv, xs, w, b, l, n_exp, tm):
    p_max, d = xs.shape
    de = w.shape[2] // 2
    tn = _pick(de, 512)
    nj = de // tn
    xmap = lambda j, i, be, rb, nv_: (rb[i], 0)
    return pl.pallas_call(
        _moe_gu_body,
        grid_spec=pltpu.PrefetchScalarGridSpec(
            num_scalar_prefetch=3,
            grid=(nj, p_max // tm),
            in_specs=[pl.BlockSpec((tm, d), xmap),
                      pl.BlockSpec((1, d, tn), lambda j, i, be, rb, nv_: (l * n_exp + be[i], 0, j)),
                      pl.BlockSpec((1, d, tn), lambda j, i, be, rb, nv_: (l * n_exp + be[i], 0, nj + j)),
                      pl.BlockSpec((1, 1, tn), lambda j, i, be, rb, nv_: (l * n_exp + be[i], 0, j)),
                      pl.BlockSpec((1, 1, tn), lambda j, i, be, rb, nv_: (l * n_exp + be[i], 0, nj + j))],
            out_specs=pl.BlockSpec((tm, tn), lambda j, i, be, rb, nv_: (i, j)),
            scratch_shapes=[pltpu.VMEM((d, tn), BF16), pltpu.VMEM((d, tn), BF16)]),
        out_shape=jax.ShapeDtypeStruct((p_max, de), BF16),
        compiler_params=_cp(("arbitrary", "arbitrary")),
        name="moe_gate_up",
    )(blk_e, row_blk, nv, xs, w, w, b, b)


def _moe_down_body(blk_e, row_blk, nv, x_ref, w_ref, b_ref, o_ref, wb):
    i = pl.program_id(1)

    @pl.when(_expert_changed(blk_e, i))
    def _():
        wb[...] = w_ref[0].astype(BF16)

    @pl.when(i < nv[0])
    def _():
        o_ref[...] = _dot(x_ref[...], wb[...]) + b_ref[0]

    @pl.when(i >= nv[0])
    def _():
        o_ref[...] = jnp.zeros(o_ref.shape, o_ref.dtype)


def _moe_down(blk_e, row_blk, nv, act, w, b, l, n_exp, tm):
    p_max, de = act.shape
    d = w.shape[2]
    tn = _pick(d, 512)
    return pl.pallas_call(
        _moe_down_body,
        grid_spec=pltpu.PrefetchScalarGridSpec(
            num_scalar_prefetch=3,
            grid=(d // tn, p_max // tm),
            in_specs=[pl.BlockSpec((tm, de), lambda j, i, be, rb, nv_: (rb[i], 0)),
                      pl.BlockSpec((1, de, tn), lambda j, i, be, rb, nv_: (l * n_exp + be[i], 0, j)),
                      pl.BlockSpec((1, 1, tn), lambda j, i, be, rb, nv_: (l * n_exp + be[i], 0, j))],
            out_specs=pl.BlockSpec((tm, tn), lambda j, i, be, rb, nv_: (i, j)),
            scratch_shapes=[pltpu.VMEM((de, tn), BF16)]),
        out_shape=jax.ShapeDtypeStruct((p_max, d), F32),
        compiler_params=_cp(("arbitrary", "arbitrary")),
        name="moe_down",
    )(blk_e, row_blk, nv, act, w, b)


def _combine_body(dest_ref, yb_ref, x_ref, p_ref, mod_ref, o_ref, buf, sem, *, geom):
    i = pl.program_id(0)
    base = i * (ROW_BLOCK * TOP_K)
    n = ROW_BLOCK * TOP_K

    def start(s, carry):
        _row_copy(yb_ref, buf, sem, dest_ref[base + s], s).start()
        return carry

    def wait(s, carry):
        _row_copy(yb_ref, buf, sem, 0, s).wait()
        return carry

    lax.fori_loop(0, n, start, 0)
    lax.fori_loop(0, n, wait, 0)
    b = geom.bidx(i)
    p = p_ref[...]
    acc = p[:, 0:1] * buf[pl.ds(0, ROW_BLOCK), :]
    for k in range(1, TOP_K):
        acc = acc + p[:, k:k + 1] * buf[pl.ds(k * ROW_BLOCK, ROW_BLOCK), :]
    o_ref[...] = x_ref[...] + mod_ref[b, pl.ds(5, 1), :] * acc


def _moe_combine(dest_km, yb, x1, probs, mod3, geom):
    n, d = x1.shape
    return pl.pallas_call(
        functools.partial(_combine_body, geom=geom),
        grid_spec=pltpu.PrefetchScalarGridSpec(
            num_scalar_prefetch=1,
            grid=(n // ROW_BLOCK,),
            in_specs=[pl.BlockSpec(memory_space=pl.ANY),
                      pl.BlockSpec((ROW_BLOCK, d), lambda i, dest: (i, 0)),
                      pl.BlockSpec((ROW_BLOCK, LANES), lambda i, dest: (i, 0)),
                      pl.BlockSpec(mod3.shape, lambda i, dest: (0, 0, 0))],
            out_specs=pl.BlockSpec((ROW_BLOCK, d), lambda i, dest: (i, 0)),
            scratch_shapes=[pltpu.VMEM((ROW_BLOCK * TOP_K, d), F32), pltpu.SemaphoreType.DMA(())]),
        out_shape=jax.ShapeDtypeStruct((n, d), F32),
        compiler_params=_cp(("arbitrary",)),
        name="moe_combine",
    )(dest_km, yb, x1, probs, mod3)


def _moe_plan(top_idx, n_exp, tm):
    n_tok = top_idx.shape[0]
    n_slot = n_tok * TOP_K
    flat_e = top_idx.reshape(n_slot)
    onehot = (flat_e[:, None] == jnp.arange(n_exp, dtype=jnp.int32)[None, :]).astype(jnp.int32)
    csum = jnp.cumsum(onehot, axis=0)
    rank = jnp.sum(onehot * csum, axis=1) - 1
    counts = csum[-1]
    p_counts = (counts + tm - 1) // tm * tm
    p_ends = jnp.cumsum(p_counts)
    p_starts = p_ends - p_counts
    dest = (p_starts[flat_e] + rank).astype(jnp.int32)
    n_tiles = n_slot // tm + n_exp
    p_max = n_tiles * tm
    flat_tok = jnp.arange(n_slot, dtype=jnp.int32) // TOP_K
    rows = jnp.zeros((p_max,), jnp.int32).at[dest].set(flat_tok, unique_indices=True)
    tile_start = jnp.arange(n_tiles, dtype=jnp.int32) * tm
    nv = (p_ends[-1] // tm).astype(jnp.int32)
    blk_e = jnp.minimum(jnp.searchsorted(p_ends, tile_start, side="right"), n_exp - 1).astype(jnp.int32)
    last_e = blk_e[jnp.maximum(nv - 1, 0)]
    tile_id = jnp.arange(n_tiles, dtype=jnp.int32)
    blk_e = jnp.where(tile_id < nv, blk_e, last_e)
    row_blk = jnp.minimum(tile_id, jnp.maximum(nv - 1, 0))
    dest_km = dest.reshape(n_tok // ROW_BLOCK, ROW_BLOCK, TOP_K).transpose(0, 2, 1).reshape(n_slot)
    return rows, dest_km, blk_e, row_blk, nv.reshape(1), p_max


def kernel(x_prompt, x_sample, cache_k, cache_v, state_ssm_re, state_ssm_im, c_prompt, c_sample,
           w_ada, b_ada, g_norm_mix, g_norm_ffn, w_in, b_in, g_q, g_k, w_attn_up,
           ssm_a_re, ssm_a_im, ssm_log_dt, ssm_b_re, ssm_b_im, ssm_c_re, ssm_c_im, ssm_d,
           w_glu, w_out, w_router, b_router, w_gate_up, b_gate_up, w_down, b_down):
    B, T, D = x_prompt.shape
    DB, DT, _ = x_sample.shape
    depth = w_ada.shape[0]
    H, hd = cache_k.shape[3], cache_k.shape[4]
    AW = H * hd
    G, N, P = ssm_b_re.shape[1:]
    SW = G * P
    E = w_router.shape[2]
    assert hd == LANES and P == 16 and 2 * N == LANES and DT == ROW_BLOCK
    geom = _Geom(B, T, DB, DT)
    n_p, n_tok, nbp = geom.n_p, geom.n_tok, geom.nbp
    L = SSM_CHUNK
    assert T % L == 0 and DT % L == 0
    moe_tm = 256

    x = jnp.concatenate([x_prompt.reshape(n_p, D), x_sample.reshape(geom.n_s, D)], axis=0)
    c_all = jnp.concatenate([c_prompt, c_sample, jnp.zeros((nbp - geom.nb, D), F32)], axis=0)

    flat2 = lambda w: w.reshape((w.shape[0] * w.shape[1],) + w.shape[2:])
    w_ada2, w_in2, w_au2, w_glu2, w_out2, w_r2 = map(flat2, (w_ada, w_in, w_attn_up, w_glu, w_out, w_router))
    w_gu3, w_dn3 = flat2(w_gate_up), flat2(w_down)
    b_gu3 = b_gate_up.reshape(depth * E, 1, b_gate_up.shape[2])
    b_dn3 = b_down.reshape(depth * E, 1, b_down.shape[2])

    lam = jnp.stack([ssm_a_re, ssm_a_im], axis=2).reshape(depth * G, 2, N)
    lamc = jnp.stack([ssm_a_re, ssm_a_im], axis=3).reshape(depth * G, N, 2)
    ldt = ssm_log_dt.reshape(depth * G, 1, 1)
    bt = jnp.stack([ssm_b_re, ssm_b_im], axis=2)
    bt = jnp.tile(bt.transpose(0, 1, 2, 4, 3), (1, 1, 1, L, 1)).reshape(depth * G, 2, L * P, N)
    ct = jnp.stack([ssm_c_re, ssm_c_im], axis=2)
    ct = jnp.tile(ct.transpose(0, 1, 2, 4, 3), (1, 1, 1, 1, L)).reshape(depth * G, 2, N, L * P)
    dtile = jnp.tile(ssm_d.reshape(depth, G, 1, P), (1, 1, 1, L)).reshape(depth * G, 1, L * P)

    nc_p, nc_s = n_p // L, geom.n_s // L
    NC = nc_p + nc_s
    cps, css = T // L, DT // L
    lidx = jnp.concatenate([jnp.arange(nc_p, dtype=jnp.int32) % cps,
                            jnp.arange(nc_s, dtype=jnp.int32) % css]).reshape(NC, 1)
    n_steps = max(cps - 1, css - 1, 1).bit_length()
    last_rows = tuple((b + 1) * cps - 1 for b in range(B)) + tuple(nc_p + (b + 1) * css - 1 for b in range(DB))

    ks, vs, srs, sis = [], [], [], []
    for l in range(depth):
        mod = _ada(c_all, w_ada2, b_ada, l)
        mod3 = mod.reshape(nbp, 6, D)
        h = _norm_mod(x, g_norm_mix, mod3, geom, l, 0, 1)

        dn = functools.partial(_dense, h, w_in2, l, geom=geom, bias=b_in)
        (qb,) = dn(0, AW, mode="q", extra=(g_q,), extra_kinds=("row",), out_dtypes=(BF16,), name="proj_q")
        kf, kb = dn(AW, AW, mode="k", extra=(g_k,), extra_kinds=("row",), out_dtypes=(F32, BF16), name="proj_k")
        vf, vb = dn(2 * AW, AW, mode="v", out_dtypes=(F32, BF16), name="proj_v")
        (u,) = dn(3 * AW, SW, mode="plain", name="proj_u")
        (sga,) = dn(3 * AW + SW, D, mode="sig", name="proj_ga")
        (sgb,) = dn(3 * AW + SW + D, D, mode="sig", name="proj_gb")

        attn = jnp.concatenate([_sb_prompt(qb, kb, vb, B, T, H, hd),
                                _sb_sample(qb, kb, vb, cache_k, cache_v, l, DB, DT, H, hd, n_p)], axis=0)
        (ma,) = _dense(attn, w_au2, l, 0, D, mode="mul", geom=geom, extra=(sga,), extra_kinds=("tile",),
                       name="attn_up")

        ut = u.reshape(NC, L, G, P).transpose(2, 0, 1, 3).reshape(G, NC, L * P)
        s0 = jnp.concatenate([state_ssm_re[l], state_ssm_im[l]], axis=-1).transpose(1, 0, 2)
        x0rows = jnp.zeros((G, NC, 2 * N), F32).at[:, nc_p::css, :].set(s0)
        yt, fin = _ssm(ut, x0rows, lidx, lam, lamc, ldt, bt, ct, dtile, l, N, n_steps, last_rows)
        yg = yt.reshape(G, NC, L, P).transpose(1, 2, 0, 3).reshape(n_tok, SW)
        fin = fin.transpose(1, 0, 2)

        merged = _glu_merge(yg, w_glu2, l, ma, sgb, geom)
        (x1,) = _dense(merged, w_out2, l, 0, D, mode="resid", geom=geom, extra=(x, mod3),
                       extra_kinds=("tile", "mod"), name="out_proj")

        h2, top_idx, probs = _norm_router(x1, g_norm_ffn, mod3, w_r2, b_router, geom, l)
        rows, dest_km, blk_e, row_blk, nv, _ = _moe_plan(top_idx[:, :TOP_K], E, moe_tm)
        xs = _moe_gather(rows, h2, moe_tm)
        act = _moe_gate_up(blk_e, row_blk, nv, xs, w_gu3, b_gu3, l, E, moe_tm)
        yb = _moe_down(blk_e, row_blk, nv, act, w_dn3, b_dn3, l, E, moe_tm)
        x = _moe_combine(dest_km, yb, x1, probs, mod3, geom)

        ks.append(kf)
        vs.append(vf)
        srs.append(fin[..., :N])
        sis.append(fin[..., N:])

    def split(a_list, tail):
        a = jnp.stack(a_list)
        return (a[:, :n_p].reshape((depth, B, T) + tail), a[:, n_p:].reshape((depth, DB, DT) + tail))

    k_p, k_s = split(ks, (H, hd))
    v_p, v_s = split(vs, (H, hd))
    sr, si = jnp.stack(srs), jnp.stack(sis)
    return (x[:n_p].reshape(B, T, D), x[n_p:].reshape(DB, DT, D), k_p, v_p, sr[:, :B], si[:, :B],
            k_s, v_s, sr[:, B:], si[:, B:])
```

```python
import functools
import math

import jax
import jax.numpy as jnp
from jax import lax
from jax.experimental import pallas as pl
from jax.experimental.pallas import tpu as pltpu

F32 = jnp.float32
BF16 = jnp.bfloat16

NORM_EPS = 1e-6
TOP_K = 4
SWIGLU_LIMIT = 7.0
SWIGLU_ALPHA = 1.702
ROW_BLOCK = 64
SSM_CHUNK = 16
LANES = 128
VMEM_LIMIT = 56 * 1024 * 1024


def _cp(sem, vmem=VMEM_LIMIT):
    return pltpu.CompilerParams(dimension_semantics=sem, vmem_limit_bytes=vmem)


def _pick(n, pref):
    t = pref
    while t > 1 and n % t:
        t //= 2
    return t


def _sigmoid(x):
    return 1.0 / (1.0 + jnp.exp(-x))


def _dot(a, b):
    return jnp.dot(a, b, preferred_element_type=F32)


def _split(a):
    hi = a.astype(BF16)
    lo = (a - hi.astype(F32)).astype(BF16)
    return hi, lo


def _dot3(a, b):
    ah, al = _split(a)
    bh, bl = _split(b)
    return _dot(ah, bh) + (_dot(al, bh) + _dot(ah, bl))


class _Geom:
    def __init__(self, B, T, DB, DT):
        self.B, self.T, self.DB, self.DT = B, T, DB, DT
        self.n_p, self.n_s = B * T, DB * DT
        self.n_tok = self.n_p + self.n_s
        assert T % ROW_BLOCK == 0 and DT % ROW_BLOCK == 0
        self.tm = 512
        while self.n_p % self.tm or self.n_s % self.tm:
            self.tm //= 2
        assert self.tm >= ROW_BLOCK
        self.sub = self.tm // ROW_BLOCK
        self.nb = B + DB
        self.nbp = -(-self.nb // 8) * 8

    def bidx(self, blk):
        pb = self.n_p // ROW_BLOCK
        return jnp.where(blk < pb, blk // (self.T // ROW_BLOCK),
                         self.B + (blk - pb) // (self.DT // ROW_BLOCK))


def _ada_body(c_ref, w_ref, b_ref, o_ref):
    c = c_ref[...]
    s = c * _sigmoid(c)
    o_ref[...] = _dot(s.astype(BF16), w_ref[...].astype(BF16)) + b_ref[...]


def _ada(c_pad, w, b, l):
    nbp, d = c_pad.shape
    n = w.shape[1]
    tn = _pick(n, 1024)
    return pl.pallas_call(
        _ada_body,
        grid=(n // tn,),
        in_specs=[pl.BlockSpec((nbp, d), lambda j: (0, 0)),
                  pl.BlockSpec((d, tn), lambda j: (l, j)),
                  pl.BlockSpec((1, tn), lambda j: (l, j))],
        out_specs=pl.BlockSpec((nbp, tn), lambda j: (0, j)),
        out_shape=jax.ShapeDtypeStruct((nbp, n), F32),
        compiler_params=_cp(("arbitrary",)),
        name="ada_mod",
    )(c_pad, w, b)


def _norm_rows(x, g, sc, sh):
    ms = jnp.mean(x * x, axis=-1, keepdims=True)
    return (x * lax.rsqrt(ms + NORM_EPS)) * g * (1.0 + sc) + sh


def _norm_mod_body(x_ref, g_ref, mod_ref, o_ref, *, geom, i_sh, i_sc):
    i = pl.program_id(0)
    for r in range(geom.sub):
        b = geom.bidx(i * geom.sub + r)
        rows = pl.ds(r * ROW_BLOCK, ROW_BLOCK)
        h = _norm_rows(x_ref[rows, :], g_ref[...], mod_ref[b, pl.ds(i_sc, 1), :], mod_ref[b, pl.ds(i_sh, 1), :])
        o_ref[rows, :] = h.astype(o_ref.dtype)


def _norm_mod(x, g, mod3, geom, l, i_sh, i_sc):
    n, d = x.shape
    tm = geom.tm
    return pl.pallas_call(
        functools.partial(_norm_mod_body, geom=geom, i_sh=i_sh, i_sc=i_sc),
        grid=(n // tm,),
        in_specs=[pl.BlockSpec((tm, d), lambda i: (i, 0)),
                  pl.BlockSpec((1, d), lambda i: (l, 0)),
                  pl.BlockSpec(mod3.shape, lambda i: (0, 0, 0))],
        out_specs=pl.BlockSpec((tm, d), lambda i: (i, 0)),
        out_shape=jax.ShapeDtypeStruct((n, d), BF16),
        compiler_params=_cp(("arbitrary",)),
        name="norm_mod",
    )(x, g, mod3)


def _head_norm(acc, g):
    outs = []
    for c in range(acc.shape[1] // LANES):
        a = acc[:, c * LANES:(c + 1) * LANES]
        ms = jnp.mean(a * a, axis=-1, keepdims=True)
        outs.append(a * lax.rsqrt(ms + NORM_EPS) * g)
    return outs


def _dense_body(*refs, mode, geom, n_extra, n_out, has_bias):
    x_ref, w_ref = refs[0], refs[1]
    pos = 2
    b_ref = None
    if has_bias:
        b_ref = refs[pos]
        pos += 1
    extra = refs[pos:pos + n_extra]
    outs = refs[pos + n_extra:pos + n_extra + n_out]
    wb = refs[pos + n_extra + n_out]
    i = pl.program_id(1)

    @pl.when(i == 0)
    def _():
        wb[...] = w_ref[...].astype(BF16)

    acc = _dot(x_ref[...], wb[...])
    if has_bias:
        acc = acc + b_ref[...]
    if mode == "q":
        qscale = LANES ** -0.5 * LOG2_E
        for c, a in enumerate(_head_norm(acc, extra[0][...])):
            outs[0][:, c * LANES:(c + 1) * LANES] = (a * qscale).astype(BF16)
    elif mode == "k":
        for c, a in enumerate(_head_norm(acc, extra[0][...])):
            outs[0][:, c * LANES:(c + 1) * LANES] = a
            outs[1][:, c * LANES:(c + 1) * LANES] = a.astype(BF16)
    elif mode == "v":
        outs[0][...] = acc
        outs[1][...] = acc.astype(BF16)
    elif mode == "plain":
        outs[0][...] = acc
    elif mode == "sig":
        outs[0][...] = _sigmoid(acc)
    elif mode == "mul":
        outs[0][...] = acc * extra[0][...]
    elif mode == "resid":
        xres, mod_ref = extra
        for r in range(geom.sub):
            b = geom.bidx(i * geom.sub + r)
            rows = pl.ds(r * ROW_BLOCK, ROW_BLOCK)
            gate = mod_ref[b, pl.ds(2, 1), :]
            outs[0][rows, :] = xres[rows, :] + gate * acc[r * ROW_BLOCK:(r + 1) * ROW_BLOCK, :]
    else:
        raise ValueError(mode)


def _dense(x, w, l, col0, ncols, *, mode, geom, bias=None, extra=(), extra_kinds=(), out_dtypes=(F32,),
           tn_pref=1024, name="dense"):
    m, k = x.shape
    tm = geom.tm
    tn = _pick(math.gcd(ncols, col0) if col0 else ncols, tn_pref)
    cb = col0 // tn
    in_specs = [pl.BlockSpec((tm, k), lambda j, i: (i, 0)),
                pl.BlockSpec((k, tn), lambda j, i: (l, cb + j), pipeline_mode=pl.Buffered(1))]
    args = [x, w]
    if bias is not None:
        in_specs.append(pl.BlockSpec((1, tn), lambda j, i: (l, cb + j)))
        args.append(bias)
    for a, kind in zip(extra, extra_kinds):
        if kind == "tile":
            in_specs.append(pl.BlockSpec((tm, tn), lambda j, i: (i, j)))
        elif kind == "row":
            in_specs.append(pl.BlockSpec((1, a.shape[1]), lambda j, i: (l, 0)))
        elif kind == "mod":
            in_specs.append(pl.BlockSpec((a.shape[0], a.shape[1], tn), lambda j, i: (0, 0, j)))
        args.append(a)
    out_specs = [pl.BlockSpec((tm, tn), lambda j, i: (i, j)) for _ in out_dtypes]
    out_shape = [jax.ShapeDtypeStruct((m, ncols), dt) for dt in out_dtypes]
    res = pl.pallas_call(
        functools.partial(_dense_body, mode=mode, geom=geom, n_extra=len(extra), n_out=len(out_dtypes),
                          has_bias=bias is not None),
        grid=(ncols // tn, m // tm),
        in_specs=in_specs,
        out_specs=out_specs,
        out_shape=out_shape,
        scratch_shapes=[pltpu.VMEM((k, tn), BF16)],
        compiler_params=_cp(("arbitrary", "arbitrary")),
        name=name,
    )(*args)
    return res


def _glu_body(x_ref, wv_ref, wg_ref, ma_ref, sgb_ref, o_ref, wvb, wgb):
    @pl.when(pl.program_id(1) == 0)
    def _():
        wvb[...] = wv_ref[...].astype(BF16)
        wgb[...] = wg_ref[...].astype(BF16)

    x = x_ref[...]
    val = _dot(x, wvb[...])
    gate = _dot(x, wgb[...])
    o_ref[...] = (ma_ref[...] + sgb_ref[...] * (val * _sigmoid(gate))).astype(o_ref.dtype)


def _glu_merge(yg, w, l, ma, sgb, geom):
    m, k = yg.shape
    d = ma.shape[1]
    tm = geom.tm
    tn = _pick(d, 1024)
    nj = d // tn
    return pl.pallas_call(
        _glu_body,
        grid=(nj, m // tm),
        in_specs=[pl.BlockSpec((tm, k), lambda j, i: (i, 0)),
                  pl.BlockSpec((k, tn), lambda j, i: (l, j), pipeline_mode=pl.Buffered(1)),
                  pl.BlockSpec((k, tn), lambda j, i: (l, nj + j), pipeline_mode=pl.Buffered(1)),
                  pl.BlockSpec((tm, tn), lambda j, i: (i, j)),
                  pl.BlockSpec((tm, tn), lambda j, i: (i, j))],
        out_specs=pl.BlockSpec((tm, tn), lambda j, i: (i, j)),
        out_shape=jax.ShapeDtypeStruct((m, d), BF16),
        scratch_shapes=[pltpu.VMEM((k, tn), BF16), pltpu.VMEM((k, tn), BF16)],
        compiler_params=_cp(("arbitrary", "arbitrary")),
        name="glu_merge",
    )(yg, w, w, ma, sgb)


LOG2_E = 1.4426950408889634


def _neg_abs(z):
    return lax.bitcast_convert_type(lax.bitcast_convert_type(z, jnp.uint32) | jnp.uint32(0x80000000), F32)


def _sb_tile(q, kt, vt, tri, mask, c, o):
    z = lax.dot_general(q, kt, (((1,), (1,)), ((), ())), preferred_element_type=F32)
    sp = jnp.maximum(z, 0.0) + jnp.log2(1.0 + jnp.exp2(_neg_abs(z)))
    if mask is not None:
        sp = jnp.where(mask, sp, 0.0)
    incl = _dot(sp.astype(BF16), tri)
    w = jnp.exp2(z - incl - c)
    if mask is not None:
        w = jnp.where(mask, w, 0.0)
    o = o + _dot(w.astype(BF16), vt)
    c = c + incl[:, 0:1]
    return c, o


def _suffix_matrix(tk):
    j = lax.broadcasted_iota(jnp.int32, (tk, tk), 0)
    s = lax.broadcasted_iota(jnp.int32, (tk, tk), 1)
    return (j >= s).astype(BF16)


def _sb_prompt_body(q_ref, k_ref, v_ref, o_ref, *, tq, tk):
    qi = pl.program_id(2)
    q = q_ref[...]
    r = tq // tk
    tri = _suffix_matrix(tk)
    row = lax.broadcasted_iota(jnp.int32, (tq, tk), 0)
    col = lax.broadcasted_iota(jnp.int32, (tq, tk), 1)
    c = jnp.zeros((tq, 1), F32)
    o = jnp.zeros((tq, q.shape[1]), F32)
    for d in reversed(range(r)):
        ks = pl.ds(pl.multiple_of((qi * r + d) * tk, tk), tk)
        c, o = _sb_tile(q, k_ref[ks, :], v_ref[ks, :], tri, col + d * tk < row, c, o)

    def body(it, carry):
        c, o = carry
        for u in range(r):
            ks = pl.ds(pl.multiple_of((qi * r - 1 - (it * r + u)) * tk, tk), tk)
            c, o = _sb_tile(q, k_ref[ks, :], v_ref[ks, :], tri, None, c, o)
        return c, o

    c, o = lax.fori_loop(0, qi, body, (c, o))
    o_ref[...] = o.astype(o_ref.dtype)


def _sb_prompt(q, kb, vb, B, T, H, hd):
    tq = _pick(T, 512)
    tk = _pick(tq, 256)
    nq = T // tq
    return pl.pallas_call(
        functools.partial(_sb_prompt_body, tq=tq, tk=tk),
        grid=(B, H, nq),
        in_specs=[pl.BlockSpec((tq, hd), lambda b, h, i: (b * nq + i, h)),
                  pl.BlockSpec((T, hd), lambda b, h, i: (b, h)),
                  pl.BlockSpec((T, hd), lambda b, h, i: (b, h))],
        out_specs=pl.BlockSpec((tq, hd), lambda b, h, i: (b * nq + i, h)),
        out_shape=jax.ShapeDtypeStruct((B * T, H * hd), BF16),
        compiler_params=_cp(("arbitrary", "arbitrary", "arbitrary")),
        name="sb_prompt",
    )(q, kb, vb)


def _sb_sample_body(q_ref, k_ref, v_ref, ck_ref, cv_ref, o_ref, *, dt, tk, past, hb, hd):
    row = lax.broadcasted_iota(jnp.int32, (dt, dt), 0)
    col = lax.broadcasted_iota(jnp.int32, (dt, dt), 1)
    tri_new = _suffix_matrix(dt)
    tri = _suffix_matrix(tk)
    n_t = past // tk
    heads = [slice(h * hd, (h + 1) * hd) for h in range(hb)]
    c0 = jnp.zeros((dt, 1), F32)
    o0 = jnp.zeros((dt, hd), F32)
    carry = tuple(_sb_tile(q_ref[:, sl], k_ref[:, sl], v_ref[:, sl], tri_new, col < row, c0, o0)
                  for sl in heads)

    def body(it, carry):
        ks = pl.ds(pl.multiple_of((n_t - 1 - it) * tk, tk), tk)
        return tuple(_sb_tile(q_ref[:, sl], ck_ref[0, ks, sl].astype(BF16), cv_ref[0, ks, sl].astype(BF16),
                              tri, None, co[0], co[1])
                     for sl, co in zip(heads, carry))

    carry = lax.fori_loop(0, n_t, body, carry)
    for sl, co in zip(heads, carry):
        o_ref[:, sl] = co[1].astype(o_ref.dtype)


def _sb_sample(q, kb, vb, ck, cv, l, DB, DT, H, hd, n_p):
    past = ck.shape[2]
    tk = _pick(past, 256)
    hb = _pick(H, 4)
    rb0 = n_p // DT
    ck3 = ck.reshape(ck.shape[0] * DB, past, H * hd)
    cv3 = cv.reshape(cv.shape[0] * DB, past, H * hd)
    new = lambda b, h: (rb0 + b, h)
    return pl.pallas_call(
        functools.partial(_sb_sample_body, dt=DT, tk=tk, past=past, hb=hb, hd=hd),
        grid=(DB, H // hb),
        in_specs=[pl.BlockSpec((DT, hb * hd), new),
                  pl.BlockSpec((DT, hb * hd), new),
                  pl.BlockSpec((DT, hb * hd), new),
                  pl.BlockSpec((1, past, hb * hd), lambda b, h: (l * DB + b, 0, h)),
                  pl.BlockSpec((1, past, hb * hd), lambda b, h: (l * DB + b, 0, h))],
        out_specs=pl.BlockSpec((DT, hb * hd), lambda b, h: (b, h)),
        out_shape=jax.ShapeDtypeStruct((DB * DT, H * hd), BF16),
        compiler_params=_cp(("arbitrary", "arbitrary")),
        name="sb_sample",
    )(q, kb, vb, ck3, cv3)


def _cexp(lr, li, t):
    mag = jnp.exp(lr * t)
    return mag * jnp.cos(li * t), mag * jnp.sin(li * t)


def _ssm_body(ut_ref, x0_ref, lidx_ref, lam_ref, lamc_ref, ldt_ref, bt_ref, ct_ref, d_ref,
              y_ref, fin_ref, e_scr, *, n_state, n_steps, last_rows):
    L = SSM_CHUNK
    N = n_state
    P = 16
    W = L * P
    dt = jnp.exp(ldt_ref[0])
    lr, li = lam_ref[0, 0:1, :], lam_ref[0, 1:2, :]
    lrc, lic = lamc_ref[0, :, 0:1], lamc_ref[0, :, 1:2]

    abr, abi = _cexp(lr, li, dt)
    den = lr * lr + li * li
    nr = abr - 1.0
    cor = (nr * lr + abi * li) / den
    coi = (abi * lr - nr * li) / den
    br, bi = bt_ref[0, 0], bt_ref[0, 1]
    bbr = cor * br - coi * bi
    bbi = cor * bi + coi * br
    jrow = (lax.broadcasted_iota(jnp.int32, (W, N), 0) // P).astype(F32)
    pr, pi = _cexp(lr, li, -jrow * dt)
    xr = bbr * pr - bbi * pi
    xi = bbr * pi + bbi * pr
    qr, qi = _cexp(lr, li, (float(L - 1) - jrow) * dt)
    wm = jnp.concatenate([bbr * qr - bbi * qi, bbr * qi + bbi * qr], axis=1)

    cr, ci = ct_ref[0, 0], ct_ref[0, 1]
    icol = (lax.broadcasted_iota(jnp.int32, (N, W), 1) // P).astype(F32)
    ar, ai = _cexp(lrc, lic, icol * dt)
    yr = cr * ar - ci * ai
    yi = cr * ai + ci * ar
    ar1, ai1 = _cexp(lrc, lic, (icol + 1.0) * dt)
    vm = jnp.concatenate([cr * ar1 - ci * ai1, -(cr * ai1 + ci * ar1)], axis=0)

    tm = _dot3(jnp.concatenate([xr, xi], axis=1), jnp.concatenate([yr, -yi], axis=0))
    rr = lax.broadcasted_iota(jnp.int32, (W, W), 0)
    cc = lax.broadcasted_iota(jnp.int32, (W, W), 1)
    tm = jnp.where(cc // P >= rr // P, tm, 0.0) + jnp.where(rr == cc, d_ref[0], 0.0)

    u = ut_ref[0]
    s_loc = _dot3(u, wm)

    def cmul(t, s):
        pr_, pi_ = _cexp(lr, li, t * dt)
        a_r = jnp.concatenate([pr_, pr_], axis=1)
        a_i = jnp.concatenate([-pi_, pi_], axis=1)
        return a_r * s + a_i * pltpu.roll(s, N, axis=1)

    lidx = lidx_ref[...]
    x0 = x0_ref[0]
    e = s_loc + cmul(float(L), x0)
    for kk in range(n_steps):
        sh = pltpu.roll(e, 1 << kk, axis=0)
        e = e + jnp.where(lidx >= (1 << kk), cmul(float(L * (1 << kk)), sh), 0.0)
    s_in = jnp.where(lidx >= 1, pltpu.roll(e, 1, axis=0), x0)
    y = _dot3(u, tm) + _dot3(s_in, vm)
    y_ref[0] = (0.5 * y * (1.0 + lax.erf(y * (2.0 ** -0.5)))).astype(y_ref.dtype)
    e_scr[...] = e
    for r, src in enumerate(last_rows):
        fin_ref[0, pl.ds(r, 1), :] = e_scr[pl.ds(src, 1), :]


def _ssm(ut, x0rows, lidx, lam, lamc, ldt, bt, ct, dtile, l, n_state, n_steps, last_rows):
    G, NC, W = ut.shape
    nb = len(last_rows)
    g_of = lambda g: (l * G + g, 0, 0)
    g4 = lambda g: (l * G + g, 0, 0, 0)
    return pl.pallas_call(
        functools.partial(_ssm_body, n_state=n_state, n_steps=n_steps, last_rows=last_rows),
        grid=(G,),
        in_specs=[pl.BlockSpec((1, NC, W), lambda g: (g, 0, 0)),
                  pl.BlockSpec((1, NC, 2 * n_state), lambda g: (g, 0, 0)),
                  pl.BlockSpec((NC, 1), lambda g: (0, 0)),
                  pl.BlockSpec((1, 2, n_state), g_of),
                  pl.BlockSpec((1, n_state, 2), g_of),
                  pl.BlockSpec((1, 1, 1), g_of),
                  pl.BlockSpec((1, 2, W, n_state), g4),
                  pl.BlockSpec((1, 2, n_state, W), g4),
                  pl.BlockSpec((1, 1, W), g_of)],
        out_specs=[pl.BlockSpec((1, NC, W), lambda g: (g, 0, 0)),
                   pl.BlockSpec((1, nb, 2 * n_state), lambda g: (g, 0, 0))],
        out_shape=[jax.ShapeDtypeStruct((G, NC, W), BF16),
                   jax.ShapeDtypeStruct((G, nb, 2 * n_state), F32)],
        scratch_shapes=[pltpu.VMEM((NC, 2 * n_state), F32)],
        compiler_params=_cp(("arbitrary",)),
        name="s5_group",
    )(ut, x0rows, lidx, lam, lamc, ldt, bt, ct, dtile)


def _norm_router_body(x_ref, g_ref, mod_ref, wr_ref, br_ref, h_ref, idx_ref, p_ref, *, geom, n_exp):
    i = pl.program_id(0)
    for r in range(geom.sub):
        b = geom.bidx(i * geom.sub + r)
        rows = pl.ds(r * ROW_BLOCK, ROW_BLOCK)
        h_ref[rows, :] = _norm_rows(x_ref[rows, :], g_ref[...], mod_ref[b, pl.ds(4, 1), :],
                                    mod_ref[b, pl.ds(3, 1), :])
    logits = _dot3(h_ref[...], wr_ref[...]) + br_ref[...]
    tm = logits.shape[0]
    lane = lax.broadcasted_iota(jnp.int32, (tm, n_exp), 1)
    wide = lax.broadcasted_iota(jnp.int32, (tm, LANES), 1)
    idx_out = jnp.zeros((tm, LANES), jnp.int32)
    vals = []
    for k in range(TOP_K):
        m = jnp.max(logits, axis=-1, keepdims=True)
        sel = jnp.min(jnp.where(logits == m, lane, n_exp), axis=-1, keepdims=True)
        idx_out = jnp.where(wide == k, sel, idx_out)
        vals.append(m)
        logits = jnp.where(lane == sel, -jnp.inf, logits)
    es = [jnp.exp(v - vals[0]) for v in vals]
    tot = es[0] + es[1] + es[2] + es[3]
    p_out = jnp.zeros((tm, LANES), F32)
    for k in range(TOP_K):
        p_out = jnp.where(wide == k, es[k] / tot, p_out)
    idx_ref[...] = idx_out
    p_ref[...] = p_out


def _norm_router(x, g, mod3, wr, br, geom, l):
    n, d = x.shape
    tm = geom.tm
    n_exp = wr.shape[1]
    return pl.pallas_call(
        functools.partial(_norm_router_body, geom=geom, n_exp=n_exp),
        grid=(n // tm,),
        in_specs=[pl.BlockSpec((tm, d), lambda i: (i, 0)),
                  pl.BlockSpec((1, d), lambda i: (l, 0)),
                  pl.BlockSpec(mod3.shape, lambda i: (0, 0, 0)),
                  pl.BlockSpec((d, n_exp), lambda i: (l, 0)),
                  pl.BlockSpec((1, n_exp), lambda i: (l, 0))],
        out_specs=[pl.BlockSpec((tm, d), lambda i: (i, 0)),
                   pl.BlockSpec((tm, LANES), lambda i: (i, 0)),
                   pl.BlockSpec((tm, LANES), lambda i: (i, 0))],
        out_shape=[jax.ShapeDtypeStruct((n, d), F32),
                   jax.ShapeDtypeStruct((n, LANES), jnp.int32),
                   jax.ShapeDtypeStruct((n, LANES), F32)],
        compiler_params=_cp(("arbitrary",)),
        name="norm_router",
    )(x, g, mod3, wr, br)


def _row_copy(src, dst, sem, s_row, d_slot):
    return pltpu.make_async_copy(src.at[pl.ds(s_row, 1), :], dst.at[pl.ds(d_slot, 1), :], sem)


DMA_UNROLL = 8


def _row_gather_step(idx_ref, src_ref, buf, sem, n_rows, n_steps):
    i = pl.program_id(0)

    def issue(step, slot):
        def start(r, carry):
            _row_copy(src_ref, buf.at[slot], sem.at[slot], idx_ref[step * n_rows + r], r).start()
            return carry

        lax.fori_loop(0, n_rows, start, 0, unroll=DMA_UNROLL)

    @pl.when(i == 0)
    def _():
        issue(0, 0)

    @pl.when(i + 1 < n_steps)
    def _():
        issue(i + 1, (i + 1) % 2)

    slot = i % 2

    def wait(r, carry):
        _row_copy(src_ref, buf.at[slot], sem.at[slot], 0, r).wait()
        return carry

    lax.fori_loop(0, n_rows, wait, 0, unroll=DMA_UNROLL)
    return slot


def _gather_body(rows_ref, src_ref, o_ref, buf, sem, *, n_rows, n_steps):
    slot = _row_gather_step(rows_ref, src_ref, buf, sem, n_rows, n_steps)
    o_ref[...] = buf[slot].astype(o_ref.dtype)


def _moe_gather(rows, h2, tm):
    p_max = rows.shape[0]
    d = h2.shape[1]
    return pl.pallas_call(
        functools.partial(_gather_body, n_rows=tm, n_steps=p_max // tm),
        grid_spec=pltpu.PrefetchScalarGridSpec(
            num_scalar_prefetch=1,
            grid=(p_max // tm,),
            in_specs=[pl.BlockSpec(memory_space=pl.ANY)],
            out_specs=pl.BlockSpec((tm, d), lambda i, rows: (i, 0)),
            scratch_shapes=[pltpu.VMEM((2, tm, d), F32), pltpu.SemaphoreType.DMA((2,))]),
        out_shape=jax.ShapeDtypeStruct((p_max, d), BF16),
        compiler_params=_cp(("arbitrary",)),
        name="moe_gather",
    )(rows, h2)


def _expert_weights(plan, w_hbm, wbuf, wbf, sem, col_blocks, tn, e0, n_pass):
    blk_e, _, first, grp, e_next, meta = plan
    j = pl.program_id(0)
    i = pl.program_id(1)
    n_grp = meta[1]

    def copies(slot, e_idx, jj):
        return [pltpu.make_async_copy(w_hbm.at[e0 + e_idx, :, pl.ds(cb * tn, tn)], wbuf.at[slot, p],
                                      sem.at[slot, p]) for p, cb in enumerate(col_blocks(jj))]

    @pl.when(first[i] == 1)
    def _():
        g = grp[i]
        gg = j * n_grp + g
        slot = gg % 2

        @pl.when(gg == 0)
        def _():
            for c in copies(0, blk_e[0], 0):
                c.start()

        for c in copies(slot, blk_e[i], j):
            c.wait()
        last_in_pass = g == n_grp - 1

        @pl.when(jnp.logical_not(last_in_pass & (j == n_pass - 1)))
        def _():
            for c in copies(1 - slot, e_next[i], jnp.where(last_in_pass, j + 1, j)):
                c.start()

        for p in range(wbf.shape[0]):
            wbf[p] = wbuf[slot, p].astype(BF16)


def _moe_gu_body(*refs, tn, e0, nj):
    plan, (x_ref, w_hbm, bg_ref, bu_ref, o_ref, wbuf, wbf, sem) = refs[:6], refs[6:]
    i = pl.program_id(1)
    nv = plan[5][0]
    _expert_weights(plan, w_hbm, wbuf, wbf, sem, lambda jj: (jj, nj + jj), tn, e0, nj)

    @pl.when(i < nv)
    def _():
        x = x_ref[...]
        g = jnp.minimum(_dot(x, wbf[0]) + bg_ref[0], SWIGLU_LIMIT)
        u = jnp.clip(_dot(x, wbf[1]) + bu_ref[0], -SWIGLU_LIMIT, SWIGLU_LIMIT)
        o_ref[...] = ((u + 1.0) * g * _sigmoid(SWIGLU_ALPHA * g)).astype(o_ref.dtype)

    @pl.when(i >= nv)
    def _():
        o_ref[...] = jnp.zeros(o_ref.shape, o_ref.dtype)


def _moe_gate_up(plan, xs, w, b, l, n_exp, tm):
    p_max, d = xs.shape
    de = w.shape[2] // 2
    tn = _pick(de, 512)
    nj = de // tn
    e0 = l * n_exp
    return pl.pallas_call(
        functools.partial(_moe_gu_body, tn=tn, e0=e0, nj=nj),
        grid_spec=pltpu.PrefetchScalarGridSpec(
            num_scalar_prefetch=6,
            grid=(nj, p_max // tm),
            in_specs=[pl.BlockSpec((tm, d), lambda j, i, be, rb, *_: (rb[i], 0)),
                      pl.BlockSpec(memory_space=pl.ANY),
                      pl.BlockSpec((1, 1, tn), lambda j, i, be, *_: (e0 + be[i], 0, j)),
                      pl.BlockSpec((1, 1, tn), lambda j, i, be, *_: (e0 + be[i], 0, nj + j))],
            out_specs=pl.BlockSpec((tm, tn), lambda j, i, *_: (i, j)),
            scratch_shapes=[pltpu.VMEM((2, 2, d, tn), F32), pltpu.VMEM((2, d, tn), BF16),
                            pltpu.SemaphoreType.DMA((2, 2))]),
        out_shape=jax.ShapeDtypeStruct((p_max, de), BF16),
        compiler_params=_cp(("arbitrary", "arbitrary")),
        name="moe_gate_up",
    )(*plan, xs, w, b, b)


def _moe_down_body(*refs, tn, e0, nj):
    plan, (x_ref, w_hbm, b_ref, o_ref, wbuf, wbf, sem) = refs[:6], refs[6:]
    i = pl.program_id(1)
    nv = plan[5][0]
    _expert_weights(plan, w_hbm, wbuf, wbf, sem, lambda jj: (jj,), tn, e0, nj)

    @pl.when(i < nv)
    def _():
        o_ref[...] = _dot(x_ref[...], wbf[0]) + b_ref[0]

    @pl.when(i >= nv)
    def _():
        o_ref[...] = jnp.zeros(o_ref.shape, o_ref.dtype)


def _moe_down(plan, act, w, b, l, n_exp, tm):
    p_max, de = act.shape
    d = w.shape[2]
    tn = _pick(d, 1024)
    nj = d // tn
    e0 = l * n_exp
    return pl.pallas_call(
        functools.partial(_moe_down_body, tn=tn, e0=e0, nj=nj),
        grid_spec=pltpu.PrefetchScalarGridSpec(
            num_scalar_prefetch=6,
            grid=(nj, p_max // tm),
            in_specs=[pl.BlockSpec((tm, de), lambda j, i, be, rb, *_: (rb[i], 0)),
                      pl.BlockSpec(memory_space=pl.ANY),
                      pl.BlockSpec((1, 1, tn), lambda j, i, be, *_: (e0 + be[i], 0, j))],
            out_specs=pl.BlockSpec((tm, tn), lambda j, i, *_: (i, j)),
            scratch_shapes=[pltpu.VMEM((2, 1, de, tn), F32), pltpu.VMEM((1, de, tn), BF16),
                            pltpu.SemaphoreType.DMA((2, 1))]),
        out_shape=jax.ShapeDtypeStruct((p_max, d), F32),
        compiler_params=_cp(("arbitrary", "arbitrary")),
        name="moe_down",
    )(*plan, act, w, b)


def _combine_body(dest_ref, yb_ref, x_ref, p_ref, mod_ref, o_ref, buf, sem, *, geom, n_steps):
    i = pl.program_id(0)
    slot = _row_gather_step(dest_ref, yb_ref, buf, sem, ROW_BLOCK * TOP_K, n_steps)
    b = geom.bidx(i)
    p = p_ref[...]
    acc = p[:, 0:1] * buf[slot, pl.ds(0, ROW_BLOCK), :]
    for k in range(1, TOP_K):
        acc = acc + p[:, k:k + 1] * buf[slot, pl.ds(k * ROW_BLOCK, ROW_BLOCK), :]
    o_ref[...] = x_ref[...] + mod_ref[b, pl.ds(5, 1), :] * acc


def _moe_combine(dest_km, yb, x1, probs, mod3, geom):
    n, d = x1.shape
    return pl.pallas_call(
        functools.partial(_combine_body, geom=geom, n_steps=n // ROW_BLOCK),
        grid_spec=pltpu.PrefetchScalarGridSpec(
            num_scalar_prefetch=1,
            grid=(n // ROW_BLOCK,),
            in_specs=[pl.BlockSpec(memory_space=pl.ANY),
                      pl.BlockSpec((ROW_BLOCK, d), lambda i, dest: (i, 0)),
                      pl.BlockSpec((ROW_BLOCK, LANES), lambda i, dest: (i, 0)),
                      pl.BlockSpec(mod3.shape, lambda i, dest: (0, 0, 0))],
            out_specs=pl.BlockSpec((ROW_BLOCK, d), lambda i, dest: (i, 0)),
            scratch_shapes=[pltpu.VMEM((2, ROW_BLOCK * TOP_K, d), F32), pltpu.SemaphoreType.DMA((2,))]),
        out_shape=jax.ShapeDtypeStruct((n, d), F32),
        compiler_params=_cp(("arbitrary",)),
        name="moe_combine",
    )(dest_km, yb, x1, probs, mod3)


def _moe_plan(top_idx, n_exp, tm):
    n_tok = top_idx.shape[0]
    n_slot = n_tok * TOP_K
    flat_e = top_idx.reshape(n_slot)
    onehot = (flat_e[:, None] == jnp.arange(n_exp, dtype=jnp.int32)[None, :]).astype(jnp.int32)
    csum = jnp.cumsum(onehot, axis=0)
    rank = jnp.sum(onehot * csum, axis=1) - 1
    counts = csum[-1]
    p_counts = (counts + tm - 1) // tm * tm
    p_ends = jnp.cumsum(p_counts)
    p_starts = p_ends - p_counts
    dest = (p_starts[flat_e] + rank).astype(jnp.int32)
    n_tiles = n_slot // tm + n_exp
    p_max = n_tiles * tm
    flat_tok = jnp.arange(n_slot, dtype=jnp.int32) // TOP_K
    rows = jnp.zeros((p_max,), jnp.int32).at[dest].set(flat_tok, unique_indices=True)
    tile_start = jnp.arange(n_tiles, dtype=jnp.int32) * tm
    nv = (p_ends[-1] // tm).astype(jnp.int32)
    blk_e = jnp.minimum(jnp.searchsorted(p_ends, tile_start, side="right"), n_exp - 1).astype(jnp.int32)
    last_e = blk_e[jnp.maximum(nv - 1, 0)]
    tile_id = jnp.arange(n_tiles, dtype=jnp.int32)
    blk_e = jnp.where(tile_id < nv, blk_e, last_e)
    row_blk = jnp.minimum(tile_id, jnp.maximum(nv - 1, 0))
    first = (tile_id < nv) & ((tile_id == 0) | (blk_e != jnp.roll(blk_e, 1)))
    grp = jnp.cumsum(first.astype(jnp.int32)) - 1
    n_grp = jnp.sum(first.astype(jnp.int32))
    starts = jnp.sort(jnp.where(first, tile_id, n_tiles))
    grp_e = blk_e[jnp.minimum(starts, n_tiles - 1)]
    e_next = grp_e[(grp + 1) % n_grp]
    meta = jnp.stack([nv, n_grp]).astype(jnp.int32)
    plan = (blk_e, row_blk, first.astype(jnp.int32), grp.astype(jnp.int32), e_next.astype(jnp.int32), meta)
    dest_km = dest.reshape(n_tok // ROW_BLOCK, ROW_BLOCK, TOP_K).transpose(0, 2, 1).reshape(n_slot)
    return rows, dest_km, plan


def kernel(x_prompt, x_sample, cache_k, cache_v, state_ssm_re, state_ssm_im, c_prompt, c_sample,
           w_ada, b_ada, g_norm_mix, g_norm_ffn, w_in, b_in, g_q, g_k, w_attn_up,
           ssm_a_re, ssm_a_im, ssm_log_dt, ssm_b_re, ssm_b_im, ssm_c_re, ssm_c_im, ssm_d,
           w_glu, w_out, w_router, b_router, w_gate_up, b_gate_up, w_down, b_down):
    B, T, D = x_prompt.shape
    DB, DT, _ = x_sample.shape
    depth = w_ada.shape[0]
    H, hd = cache_k.shape[3], cache_k.shape[4]
    AW = H * hd
    G, N, P = ssm_b_re.shape[1:]
    SW = G * P
    E = w_router.shape[2]
    assert hd == LANES and P == 16 and 2 * N == LANES and DT == ROW_BLOCK
    geom = _Geom(B, T, DB, DT)
    n_p, n_tok, nbp = geom.n_p, geom.n_tok, geom.nbp
    L = SSM_CHUNK
    assert T % L == 0 and DT % L == 0
    moe_tm = 256

    x = jnp.concatenate([x_prompt.reshape(n_p, D), x_sample.reshape(geom.n_s, D)], axis=0)
    c_all = jnp.concatenate([c_prompt, c_sample, jnp.zeros((nbp - geom.nb, D), F32)], axis=0)

    flat2 = lambda w: w.reshape((w.shape[0] * w.shape[1],) + w.shape[2:])
    w_ada2, w_in2, w_au2, w_glu2, w_out2, w_r2 = map(flat2, (w_ada, w_in, w_attn_up, w_glu, w_out, w_router))
    w_gu3, w_dn3 = flat2(w_gate_up), flat2(w_down)
    b_gu3 = b_gate_up.reshape(depth * E, 1, b_gate_up.shape[2])
    b_dn3 = b_down.reshape(depth * E, 1, b_down.shape[2])

    lam = jnp.stack([ssm_a_re, ssm_a_im], axis=2).reshape(depth * G, 2, N)
    lamc = jnp.stack([ssm_a_re, ssm_a_im], axis=3).reshape(depth * G, N, 2)
    ldt = ssm_log_dt.reshape(depth * G, 1, 1)
    bt = jnp.stack([ssm_b_re, ssm_b_im], axis=2)
    bt = jnp.tile(bt.transpose(0, 1, 2, 4, 3), (1, 1, 1, L, 1)).reshape(depth * G, 2, L * P, N)
    ct = jnp.stack([ssm_c_re, ssm_c_im], axis=2)
    ct = jnp.tile(ct.transpose(0, 1, 2, 4, 3), (1, 1, 1, 1, L)).reshape(depth * G, 2, N, L * P)
    dtile = jnp.tile(ssm_d.reshape(depth, G, 1, P), (1, 1, 1, L)).reshape(depth * G, 1, L * P)

    nc_p, nc_s = n_p // L, geom.n_s // L
    NC = nc_p + nc_s
    cps, css = T // L, DT // L
    lidx = jnp.concatenate([jnp.arange(nc_p, dtype=jnp.int32) % cps,
                            jnp.arange(nc_s, dtype=jnp.int32) % css]).reshape(NC, 1)
    n_steps = max(cps - 1, css - 1, 1).bit_length()
    last_rows = tuple((b + 1) * cps - 1 for b in range(B)) + tuple(nc_p + (b + 1) * css - 1 for b in range(DB))

    ks, vs, srs, sis = [], [], [], []
    for l in range(depth):
        mod = _ada(c_all, w_ada2, b_ada, l)
        mod3 = mod.reshape(nbp, 6, D)
        h = _norm_mod(x, g_norm_mix, mod3, geom, l, 0, 1)

        dn = functools.partial(_dense, h, w_in2, l, geom=geom, bias=b_in)
        (qb,) = dn(0, AW, mode="q", extra=(g_q,), extra_kinds=("row",), out_dtypes=(BF16,), name="proj_q")
        kf, kb = dn(AW, AW, mode="k", extra=(g_k,), extra_kinds=("row",), out_dtypes=(F32, BF16), name="proj_k")
        vf, vb = dn(2 * AW, AW, mode="v", out_dtypes=(F32, BF16), name="proj_v")
        (u,) = dn(3 * AW, SW, mode="plain", name="proj_u")
        (sga,) = dn(3 * AW + SW, D, mode="sig", name="proj_ga")
        (sgb,) = dn(3 * AW + SW + D, D, mode="sig", name="proj_gb")

        attn = jnp.concatenate([_sb_prompt(qb, kb, vb, B, T, H, hd),
                                _sb_sample(qb, kb, vb, cache_k, cache_v, l, DB, DT, H, hd, n_p)], axis=0)
        (ma,) = _dense(attn, w_au2, l, 0, D, mode="mul", geom=geom, extra=(sga,), extra_kinds=("tile",),
                       name="attn_up")

        ut = u.reshape(NC, L, G, P).transpose(2, 0, 1, 3).reshape(G, NC, L * P)
        s0 = jnp.concatenate([state_ssm_re[l], state_ssm_im[l]], axis=-1).transpose(1, 0, 2)
        x0rows = jnp.zeros((G, NC, 2 * N), F32).at[:, nc_p::css, :].set(s0)
        yt, fin = _ssm(ut, x0rows, lidx, lam, lamc, ldt, bt, ct, dtile, l, N, n_steps, last_rows)
        yg = yt.reshape(G, NC, L, P).transpose(1, 2, 0, 3).reshape(n_tok, SW)
        fin = fin.transpose(1, 0, 2)

        merged = _glu_merge(yg, w_glu2, l, ma, sgb, geom)
        (x1,) = _dense(merged, w_out2, l, 0, D, mode="resid", geom=geom, extra=(x, mod3),
                       extra_kinds=("tile", "mod"), name="out_proj")

        h2, top_idx, probs = _norm_router(x1, g_norm_ffn, mod3, w_r2, b_router, geom, l)
        rows, dest_km, plan = _moe_plan(top_idx[:, :TOP_K], E, moe_tm)
        xs = _moe_gather(rows, h2, moe_tm)
        act = _moe_gate_up(plan, xs, w_gu3, b_gu3, l, E, moe_tm)
        yb = _moe_down(plan, act, w_dn3, b_dn3, l, E, moe_tm)
        x = _moe_combine(dest_km, yb, x1, probs, mod3, geom)

        ks.append(kf)
        vs.append(vf)
        srs.append(fin[..., :N])
        sis.append(fin[..., N:])

    def split(a_list, tail):
        a = jnp.stack(a_list)
        return (a[:, :n_p].reshape((depth, B, T) + tail), a[:, n_p:].reshape((depth, DB, DT) + tail))

    k_p, k_s = split(ks, (H, hd))
    v_p, v_s = split(vs, (H, hd))
    sr, si = jnp.stack(srs), jnp.stack(sis)
    return (x[:n_p].reshape(B, T, D), x[n_p:].reshape(DB, DT, D), k_p, v_p, sr[:, :B], si[:, :B],
            k_s, v_s, sr[:, B:], si[:, B:])
```

```python
import functools
import math

import jax
import jax.numpy as jnp
from jax import lax
from jax.experimental import pallas as pl
from jax.experimental.pallas import tpu as pltpu

F32 = jnp.float32
BF16 = jnp.bfloat16

NORM_EPS = 1e-6
TOP_K = 4
SWIGLU_LIMIT = 7.0
SWIGLU_ALPHA = 1.702
ROW_BLOCK = 64
SSM_CHUNK = 16
LANES = 128
VMEM_LIMIT = 60 * 1024 * 1024


def _cp(sem, vmem=VMEM_LIMIT):
    return pltpu.CompilerParams(dimension_semantics=sem, vmem_limit_bytes=vmem)


def _pick(n, pref):
    t = pref
    while t > 1 and n % t:
        t //= 2
    return t


def _sigmoid(x):
    return 1.0 / (1.0 + jnp.exp(-x))


def _dot(a, b):
    return jnp.dot(a, b, preferred_element_type=F32)


def _split(a):
    hi = a.astype(BF16)
    lo = (a - hi.astype(F32)).astype(BF16)
    return hi, lo


def _dot3(a, b):
    ah, al = _split(a)
    bh, bl = _split(b)
    return _dot(ah, bh) + (_dot(al, bh) + _dot(ah, bl))


class _Geom:
    def __init__(self, B, T, DB, DT):
        self.B, self.T, self.DB, self.DT = B, T, DB, DT
        self.n_p, self.n_s = B * T, DB * DT
        self.n_tok = self.n_p + self.n_s
        assert T % ROW_BLOCK == 0 and DT % ROW_BLOCK == 0
        self.tm = 512
        while self.n_p % self.tm or self.n_s % self.tm:
            self.tm //= 2
        assert self.tm >= ROW_BLOCK
        self.sub = self.tm // ROW_BLOCK
        self.npt = self.n_p // self.tm
        self.nb = B + DB
        self.nbp = -(-self.nb // 8) * 8

    def p_blk(self, i):
        return jnp.minimum(i, self.npt - 1)

    def s_blk(self, i):
        return jnp.maximum(i - self.npt, 0)

    def bidx(self, blk):
        pb = self.n_p // ROW_BLOCK
        return jnp.where(blk < pb, blk // (self.T // ROW_BLOCK),
                         self.B + (blk - pb) // (self.DT // ROW_BLOCK))


def _ada_body(c_ref, w_ref, b_ref, o_ref):
    c = c_ref[...]
    s = c * _sigmoid(c)
    o_ref[...] = _dot(s.astype(BF16), w_ref[...].astype(BF16)) + b_ref[...]


def _ada(c_pad, w, b, l):
    nbp, d = c_pad.shape
    n = w.shape[1]
    tn = _pick(n, 1024)
    return pl.pallas_call(
        _ada_body,
        grid=(n // tn,),
        in_specs=[pl.BlockSpec((nbp, d), lambda j: (0, 0)),
                  pl.BlockSpec((d, tn), lambda j: (l, j)),
                  pl.BlockSpec((1, tn), lambda j: (l, j))],
        out_specs=pl.BlockSpec((nbp, tn), lambda j: (0, j)),
        out_shape=jax.ShapeDtypeStruct((nbp, n), F32),
        compiler_params=_cp(("arbitrary",)),
        name="ada_mod",
    )(c_pad, w, b)


def _norm_rows(x, g, sc, sh):
    ms = jnp.mean(x * x, axis=-1, keepdims=True)
    return (x * lax.rsqrt(ms + NORM_EPS)) * g * (1.0 + sc) + sh


def _load_rows(i, geom, ref_p, ref_s, rows):
    return jnp.where(i < geom.npt, ref_p[rows, :], ref_s[rows, :])


def _store_tile(i, geom, ref_p, ref_s, val):
    @pl.when(i < geom.npt)
    def _():
        ref_p[...] = val

    @pl.when(i >= geom.npt)
    def _():
        ref_s[...] = val


def _norm_mod_body(xp_ref, xs_ref, g_ref, mod_ref, o_ref, *, geom, i_sh, i_sc):
    i = pl.program_id(0)
    for r in range(geom.sub):
        b = geom.bidx(i * geom.sub + r)
        rows = pl.ds(r * ROW_BLOCK, ROW_BLOCK)
        h = _norm_rows(_load_rows(i, geom, xp_ref, xs_ref, rows), g_ref[...], mod_ref[b, pl.ds(i_sc, 1), :],
                       mod_ref[b, pl.ds(i_sh, 1), :])
        o_ref[rows, :] = h.astype(o_ref.dtype)


def _norm_mod(xp, xs, g, mod3, geom, l, i_sh, i_sc):
    d = xp.shape[1]
    tm = geom.tm
    return pl.pallas_call(
        functools.partial(_norm_mod_body, geom=geom, i_sh=i_sh, i_sc=i_sc),
        grid=(geom.n_tok // tm,),
        in_specs=[pl.BlockSpec((tm, d), lambda i: (geom.p_blk(i), 0)),
                  pl.BlockSpec((tm, d), lambda i: (geom.s_blk(i), 0)),
                  pl.BlockSpec((1, d), lambda i: (l, 0)),
                  pl.BlockSpec(mod3.shape, lambda i: (0, 0, 0))],
        out_specs=pl.BlockSpec((tm, d), lambda i: (i, 0)),
        out_shape=jax.ShapeDtypeStruct((geom.n_tok, d), BF16),
        compiler_params=_cp(("arbitrary",)),
        name="norm_mod",
    )(xp, xs, g, mod3)


def _head_norm(acc, g):
    outs = []
    for c in range(acc.shape[1] // LANES):
        a = acc[:, c * LANES:(c + 1) * LANES]
        ms = jnp.mean(a * a, axis=-1, keepdims=True)
        outs.append(a * lax.rsqrt(ms + NORM_EPS) * g)
    return jnp.concatenate(outs, axis=1)


def _dense_body(*refs, mode, geom, n_extra, n_out, has_bias):
    x_ref, w_ref = refs[0], refs[1]
    pos = 2
    b_ref = None
    if has_bias:
        b_ref = refs[pos]
        pos += 1
    extra = refs[pos:pos + n_extra]
    outs = refs[pos + n_extra:pos + n_extra + n_out]
    wb = refs[pos + n_extra + n_out]
    i = pl.program_id(1)

    @pl.when(i == 0)
    def _():
        wb[...] = w_ref[...].astype(BF16)

    acc = _dot(x_ref[...], wb[...])
    if has_bias:
        acc = acc + b_ref[...]
    if mode == "q":
        qscale = LANES ** -0.5 * LOG2_E
        outs[0][...] = (_head_norm(acc, extra[0][...]) * qscale).astype(BF16)
    elif mode == "k":
        a = _head_norm(acc, extra[0][...])
        _store_tile(i, geom, outs[0], outs[1], a)
        outs[2][...] = a.astype(BF16)
    elif mode == "v":
        _store_tile(i, geom, outs[0], outs[1], acc)
        outs[2][...] = acc.astype(BF16)
    elif mode == "plain":
        outs[0][...] = acc
    elif mode == "sig":
        outs[0][...] = _sigmoid(acc)
    elif mode == "mul":
        outs[0][...] = acc * extra[0][...]
    elif mode == "resid":
        xres_p, xres_s, mod_ref = extra
        for r in range(geom.sub):
            b = geom.bidx(i * geom.sub + r)
            rows = pl.ds(r * ROW_BLOCK, ROW_BLOCK)
            gate = mod_ref[b, pl.ds(2, 1), :]
            outs[0][rows, :] = (_load_rows(i, geom, xres_p, xres_s, rows)
                                + gate * acc[r * ROW_BLOCK:(r + 1) * ROW_BLOCK, :])
    else:
        raise ValueError(mode)


def _dense(x, w, l, col0, ncols, *, mode, geom, bias=None, extra=(), extra_kinds=(), out_kinds=((F32, "all"),),
           tn_pref=1024, name="dense"):
    m, k = x.shape
    tm = geom.tm
    tn = _pick(math.gcd(ncols, col0) if col0 else ncols, tn_pref)
    cb = col0 // tn
    row_maps = {"all": lambda j, i: (i, j), "p": lambda j, i: (geom.p_blk(i), j),
                "s": lambda j, i: (geom.s_blk(i), j)}
    n_rows = {"all": m, "p": geom.n_p, "s": geom.n_s}
    in_specs = [pl.BlockSpec((tm, k), lambda j, i: (i, 0)),
                pl.BlockSpec((k, tn), lambda j, i: (l, cb + j), pipeline_mode=pl.Buffered(1))]
    args = [x, w]
    if bias is not None:
        in_specs.append(pl.BlockSpec((1, tn), lambda j, i: (l, cb + j)))
        args.append(bias)
    for a, kind in zip(extra, extra_kinds):
        if kind == "tile":
            in_specs.append(pl.BlockSpec((tm, tn), row_maps["all"]))
        elif kind in ("tile_p", "tile_s"):
            in_specs.append(pl.BlockSpec((tm, tn), row_maps[kind[-1]]))
        elif kind == "row":
            in_specs.append(pl.BlockSpec((1, a.shape[1]), lambda j, i: (l, 0)))
        elif kind == "mod":
            in_specs.append(pl.BlockSpec((a.shape[0], a.shape[1], tn), lambda j, i: (0, 0, j)))
        args.append(a)
    out_specs = [pl.BlockSpec((tm, tn), row_maps[part]) for _, part in out_kinds]
    out_shape = [jax.ShapeDtypeStruct((n_rows[part], ncols), dt) for dt, part in out_kinds]
    res = pl.pallas_call(
        functools.partial(_dense_body, mode=mode, geom=geom, n_extra=len(extra), n_out=len(out_kinds),
                          has_bias=bias is not None),
        grid=(ncols // tn, m // tm),
        in_specs=in_specs,
        out_specs=out_specs,
        out_shape=out_shape,
        scratch_shapes=[pltpu.VMEM((k, tn), BF16)],
        compiler_params=_cp(("arbitrary", "arbitrary")),
        name=name,
    )(*args)
    return res


def _glu_body(x_ref, wv_ref, wg_ref, ma_ref, sgb_ref, o_ref, wvb, wgb):
    @pl.when(pl.program_id(1) == 0)
    def _():
        wvb[...] = wv_ref[...].astype(BF16)
        wgb[...] = wg_ref[...].astype(BF16)

    x = x_ref[...].astype(BF16)
    val = _dot(x, wvb[...])
    gate = _dot(x, wgb[...])
    o_ref[...] = (ma_ref[...] + sgb_ref[...] * (val * _sigmoid(gate))).astype(o_ref.dtype)


def _glu_merge(yg, w, l, ma, sgb, geom):
    m, k = yg.shape
    d = ma.shape[1]
    tm = geom.tm
    tn = _pick(d, 1024)
    nj = d // tn
    return pl.pallas_call(
        _glu_body,
        grid=(nj, m // tm),
        in_specs=[pl.BlockSpec((tm, k), lambda j, i: (i, 0)),
                  pl.BlockSpec((k, tn), lambda j, i: (l, j), pipeline_mode=pl.Buffered(1)),
                  pl.BlockSpec((k, tn), lambda j, i: (l, nj + j), pipeline_mode=pl.Buffered(1)),
                  pl.BlockSpec((tm, tn), lambda j, i: (i, j)),
                  pl.BlockSpec((tm, tn), lambda j, i: (i, j))],
        out_specs=pl.BlockSpec((tm, tn), lambda j, i: (i, j)),
        out_shape=jax.ShapeDtypeStruct((m, d), BF16),
        scratch_shapes=[pltpu.VMEM((k, tn), BF16), pltpu.VMEM((k, tn), BF16)],
        compiler_params=_cp(("arbitrary", "arbitrary")),
        name="glu_merge",
    )(yg, w, w, ma, sgb)


LOG2_E = 1.4426950408889634


def _neg_abs(z):
    return lax.bitcast_convert_type(lax.bitcast_convert_type(z, jnp.uint32) | jnp.uint32(0x80000000), F32)


def _sb_tile(q, kt, vt, tri, mask, c, o):
    z = lax.dot_general(q, kt, (((1,), (1,)), ((), ())), preferred_element_type=F32)
    sp = jnp.maximum(z, 0.0) + jnp.log2(1.0 + jnp.exp2(_neg_abs(z)))
    if mask is not None:
        sp = jnp.where(mask, sp, 0.0)
    incl = _dot(sp.astype(BF16), tri)
    w = jnp.exp2(z - incl - c)
    if mask is not None:
        w = jnp.where(mask, w, 0.0)
    o = o + _dot(w.astype(BF16), vt)
    c = c + incl[:, 0:1]
    return c, o


def _suffix_matrix(tk):
    j = lax.broadcasted_iota(jnp.int32, (tk, tk), 0)
    s = lax.broadcasted_iota(jnp.int32, (tk, tk), 1)
    return (j >= s).astype(BF16)


def _sb_prompt_body(q_ref, k_ref, v_ref, o_ref, *, tq, tk):
    qi = pl.program_id(2)
    q = q_ref[...]
    r = tq // tk
    tri = _suffix_matrix(tk)
    row = lax.broadcasted_iota(jnp.int32, (tq, tk), 0)
    col = lax.broadcasted_iota(jnp.int32, (tq, tk), 1)
    c = jnp.zeros((tq, 1), F32)
    o = jnp.zeros((tq, q.shape[1]), F32)
    for d in reversed(range(r)):
        ks = pl.ds(pl.multiple_of((qi * r + d) * tk, tk), tk)
        c, o = _sb_tile(q, k_ref[ks, :], v_ref[ks, :], tri, col + d * tk < row, c, o)

    def body(it, carry):
        c, o = carry
        for u in range(r):
            ks = pl.ds(pl.multiple_of((qi * r - 1 - (it * r + u)) * tk, tk), tk)
            c, o = _sb_tile(q, k_ref[ks, :], v_ref[ks, :], tri, None, c, o)
        return c, o

    c, o = lax.fori_loop(0, qi, body, (c, o))
    o_ref[...] = o.astype(o_ref.dtype)


def _sb_prompt(q, kb, vb, B, T, H, hd):
    tq = _pick(T, 512)
    tk = _pick(tq, 256)
    nq = T // tq
    return pl.pallas_call(
        functools.partial(_sb_prompt_body, tq=tq, tk=tk),
        grid=(B, H, nq),
        in_specs=[pl.BlockSpec((tq, hd), lambda b, h, i: (b * nq + i, h)),
                  pl.BlockSpec((T, hd), lambda b, h, i: (b, h)),
                  pl.BlockSpec((T, hd), lambda b, h, i: (b, h))],
        out_specs=pl.BlockSpec((tq, hd), lambda b, h, i: (b * nq + i, h)),
        out_shape=jax.ShapeDtypeStruct((B * T, H * hd), BF16),
        compiler_params=_cp(("arbitrary", "arbitrary", "arbitrary")),
        name="sb_prompt",
    )(q, kb, vb)


def _sb_sample_body(q_ref, k_ref, v_ref, ck_ref, cv_ref, o_ref, *, dt, tk, past, hb, hd):
    row = lax.broadcasted_iota(jnp.int32, (dt, dt), 0)
    col = lax.broadcasted_iota(jnp.int32, (dt, dt), 1)
    tri_new = _suffix_matrix(dt)
    tri = _suffix_matrix(tk)
    n_t = past // tk
    heads = [slice(h * hd, (h + 1) * hd) for h in range(hb)]
    c0 = jnp.zeros((dt, 1), F32)
    o0 = jnp.zeros((dt, hd), F32)
    carry = tuple(_sb_tile(q_ref[:, sl], k_ref[:, sl], v_ref[:, sl], tri_new, col < row, c0, o0)
                  for sl in heads)

    def body(it, carry):
        ks = pl.ds(pl.multiple_of((n_t - 1 - it) * tk, tk), tk)
        return tuple(_sb_tile(q_ref[:, sl], ck_ref[0, ks, sl].astype(BF16), cv_ref[0, ks, sl].astype(BF16),
                              tri, None, co[0], co[1])
                     for sl, co in zip(heads, carry))

    carry = lax.fori_loop(0, n_t, body, carry)
    for sl, co in zip(heads, carry):
        o_ref[:, sl] = co[1].astype(o_ref.dtype)


def _sb_sample(q, kb, vb, ck, cv, l, DB, DT, H, hd, n_p):
    past = ck.shape[2]
    tk = _pick(past, 256)
    hb = _pick(H, 4)
    rb0 = n_p // DT
    ck3 = ck.reshape(ck.shape[0] * DB, past, H * hd)
    cv3 = cv.reshape(cv.shape[0] * DB, past, H * hd)
    new = lambda b, h: (rb0 + b, h)
    return pl.pallas_call(
        functools.partial(_sb_sample_body, dt=DT, tk=tk, past=past, hb=hb, hd=hd),
        grid=(DB, H // hb),
        in_specs=[pl.BlockSpec((DT, hb * hd), new),
                  pl.BlockSpec((DT, hb * hd), new),
                  pl.BlockSpec((DT, hb * hd), new),
                  pl.BlockSpec((1, past, hb * hd), lambda b, h: (l * DB + b, 0, h)),
                  pl.BlockSpec((1, past, hb * hd), lambda b, h: (l * DB + b, 0, h))],
        out_specs=pl.BlockSpec((DT, hb * hd), lambda b, h: (b, h)),
        out_shape=jax.ShapeDtypeStruct((DB * DT, H * hd), BF16),
        compiler_params=_cp(("arbitrary", "arbitrary")),
        name="sb_sample",
    )(q, kb, vb, ck3, cv3)


def _cexp(lr, li, t):
    mag = jnp.exp(lr * t)
    return mag * jnp.cos(li * t), mag * jnp.sin(li * t)


def _split3(a):
    hi = a.astype(BF16)
    r1 = a - hi.astype(F32)
    mid = r1.astype(BF16)
    return hi, mid, (r1 - mid.astype(F32)).astype(BF16)


NT_DIMS = (((1,), (1,)), ((), ()))


def _s5_body(u_ref, x0_ref, lidx_ref, lam_ref, ldt_ref, b_ref, c_ref, d_ref, y_ref, fin_ref,
             wst, vst, tbig, ucat, x0s, e_scr, *, n_state, n_steps, first_rows, last_rows):
    L, P, N = SSM_CHUNK, 16, n_state
    NS = (LANES // P) * N
    nc = ucat.shape[0]
    lr, li = lam_ref[0, 0:1, :], lam_ref[0, 1:2, :]
    dt = jnp.exp(ldt_ref[0])
    abr, abi = _cexp(lr, li, dt)
    den = lr * lr + li * li
    nr = abr - 1.0
    cor = (nr * lr + abi * li) / den
    coi = (abi * lr - nr * li) / den

    expand = (lax.broadcasted_iota(jnp.int32, (N, NS), 1) % N
              == lax.broadcasted_iota(jnp.int32, (N, NS), 0)).astype(BF16)
    own = (lax.broadcasted_iota(jnp.int32, (LANES, NS), 0) // P
           == lax.broadcasted_iota(jnp.int32, (LANES, NS), 1) // N)

    def block_diag(x):
        h, m, lo = _split3(x)
        return jnp.where(own, _dot(h, expand) + (_dot(m, expand) + _dot(lo, expand)), 0.0)

    br, bi = block_diag(b_ref[0, 0]), block_diag(b_ref[0, 1])
    cr, ci = block_diag(c_ref[0, 0]), block_diag(c_ref[0, 1])
    wr, wi = cor * br - coi * bi, cor * bi + coi * br
    vr, vi = cr * abr - ci * abi, cr * abi + ci * abr
    for m in range(L):
        wst[pl.ds((L - 1 - m) * LANES, LANES), :] = jnp.concatenate([wr, wi], axis=1).astype(BF16)
        vst[pl.ds(m * LANES, LANES), :] = jnp.concatenate([vr, -vi], axis=1).astype(BF16)
        if m + 1 < L:
            wr, wi = wr * abr - wi * abi, wr * abi + wi * abr
            vr, vi = vr * abr - vi * abi, vr * abi + vi * abr

    cc = jnp.concatenate([cr, -ci], axis=1).astype(BF16)
    t_all = lax.dot_general(wst[...], cc, NT_DIMS, preferred_element_type=F32)
    eye = (lax.broadcasted_iota(jnp.int32, (LANES, LANES), 0)
           == lax.broadcasted_iota(jnp.int32, (LANES, LANES), 1))
    t0 = t_all[(L - 1) * LANES:, :] + jnp.where(eye, d_ref[0], 0.0)
    t_bf = jnp.concatenate([t_all[:(L - 1) * LANES, :], t0], axis=0).astype(BF16)
    for i in range(L):
        cols = pl.ds(i * LANES, LANES)
        tbig[pl.ds(0, (i + 1) * LANES), cols] = t_bf[(L - 1 - i) * LANES:, :]
        if i + 1 < L:
            tbig[pl.ds((i + 1) * LANES, (L - 1 - i) * LANES), cols] = jnp.zeros(((L - 1 - i) * LANES, LANES), BF16)

    for j in range(L):
        ucat[:, pl.ds(j * LANES, LANES)] = u_ref[pl.ds(j, nc, stride=L), :].astype(BF16)
    uc = ucat[...]
    s_loc = _dot(uc, wst[...])

    x0s[...] = jnp.zeros(x0s.shape, F32)
    for b, row in enumerate(first_rows):
        x0s[pl.ds(row, 1), :] = x0_ref[0, pl.ds(b, 1), :]
    x0 = x0s[...]

    def cmul(t, s):
        pr, pi = _cexp(lr, li, t * dt)
        sr, si = s[:, :NS], s[:, NS:]
        return jnp.concatenate([pr * sr - pi * si, pr * si + pi * sr], axis=1)

    lidx = lidx_ref[...]
    e = s_loc + cmul(float(L), x0)
    for kk in range(n_steps):
        sh = pltpu.roll(e, 1 << kk, axis=0)
        e = e + jnp.where(lidx >= (1 << kk), cmul(float(L * (1 << kk)), sh), 0.0)
    s_in = jnp.where(lidx >= 1, pltpu.roll(e, 1, axis=0), x0)
    y = _dot(uc, tbig[...]) + lax.dot_general(s_in.astype(BF16), vst[...], NT_DIMS, preferred_element_type=F32)
    g = 0.5 * y * (1.0 + lax.erf(y * (2.0 ** -0.5)))
    for i in range(L):
        y_ref[pl.ds(i, nc, stride=L), :] = g[:, i * LANES:(i + 1) * LANES]
    e_scr[...] = e
    for r, src in enumerate(last_rows):
        fin_ref[0, pl.ds(r, 1), :] = e_scr[pl.ds(src, 1), :]


def _s5(u, x0, lidx, lam, ldt, bb, cb, db, l, n_state, n_steps, first_rows, last_rows):
    n_tok, sw = u.shape
    gb = sw // LANES
    nc = n_tok // SSM_CHUNK
    ns2 = 2 * (LANES // 16) * n_state
    nb = len(last_rows)
    w = SSM_CHUNK * LANES
    blk3 = lambda g: (l * gb + g, 0, 0)
    blk4 = lambda g: (l * gb + g, 0, 0, 0)
    return pl.pallas_call(
        functools.partial(_s5_body, n_state=n_state, n_steps=n_steps, first_rows=first_rows, last_rows=last_rows),
        grid=(gb,),
        in_specs=[pl.BlockSpec((n_tok, LANES), lambda g: (0, g)),
                  pl.BlockSpec((1, x0.shape[1], ns2), lambda g: (g, 0, 0)),
                  pl.BlockSpec((nc, 1), lambda g: (0, 0)),
                  pl.BlockSpec((1, 2, ns2 // 2), blk3),
                  pl.BlockSpec((1, 1, ns2 // 2), blk3),
                  pl.BlockSpec((1, 2, LANES, n_state), blk4),
                  pl.BlockSpec((1, 2, LANES, n_state), blk4),
                  pl.BlockSpec((1, 1, LANES), blk3)],
        out_specs=[pl.BlockSpec((n_tok, LANES), lambda g: (0, g)),
                   pl.BlockSpec((1, nb, ns2), lambda g: (g, 0, 0))],
        out_shape=[jax.ShapeDtypeStruct((n_tok, sw), F32),
                   jax.ShapeDtypeStruct((gb, nb, ns2), F32)],
        scratch_shapes=[pltpu.VMEM((w, ns2), BF16), pltpu.VMEM((w, ns2), BF16), pltpu.VMEM((w, w), BF16),
                        pltpu.VMEM((nc, w), BF16), pltpu.VMEM((nc, ns2), F32), pltpu.VMEM((nc, ns2), F32)],
        compiler_params=_cp(("arbitrary",)),
        name="s5_block",
    )(u, x0, lidx, lam, ldt, bb, cb, db)


def _norm_router_body(x_ref, g_ref, mod_ref, wr_ref, br_ref, h_ref, idx_ref, p_ref, *, geom, n_exp):
    i = pl.program_id(0)
    for r in range(geom.sub):
        b = geom.bidx(i * geom.sub + r)
        rows = pl.ds(r * ROW_BLOCK, ROW_BLOCK)
        h_ref[rows, :] = _norm_rows(x_ref[rows, :], g_ref[...], mod_ref[b, pl.ds(4, 1), :],
                                    mod_ref[b, pl.ds(3, 1), :])
    logits = _dot3(h_ref[...], wr_ref[...]) + br_ref[...]
    tm = logits.shape[0]
    lane = lax.broadcasted_iota(jnp.int32, (tm, n_exp), 1)
    wide = lax.broadcasted_iota(jnp.int32, (tm, LANES), 1)
    idx_out = jnp.zeros((tm, LANES), jnp.int32)
    vals = []
    for k in range(TOP_K):
        m = jnp.max(logits, axis=-1, keepdims=True)
        sel = jnp.min(jnp.where(logits == m, lane, n_exp), axis=-1, keepdims=True)
        idx_out = jnp.where(wide == k, sel, idx_out)
        vals.append(m)
        logits = jnp.where(lane == sel, -jnp.inf, logits)
    es = [jnp.exp(v - vals[0]) for v in vals]
    tot = es[0] + es[1] + es[2] + es[3]
    p_out = jnp.zeros((tm, LANES), F32)
    for k in range(TOP_K):
        p_out = jnp.where(wide == k, es[k] / tot, p_out)
    idx_ref[...] = idx_out
    p_ref[...] = p_out


def _norm_router(x, g, mod3, wr, br, geom, l):
    n, d = x.shape
    tm = geom.tm
    n_exp = wr.shape[1]
    return pl.pallas_call(
        functools.partial(_norm_router_body, geom=geom, n_exp=n_exp),
        grid=(n // tm,),
        in_specs=[pl.BlockSpec((tm, d), lambda i: (i, 0)),
                  pl.BlockSpec((1, d), lambda i: (l, 0)),
                  pl.BlockSpec(mod3.shape, lambda i: (0, 0, 0)),
                  pl.BlockSpec((d, n_exp), lambda i: (l, 0)),
                  pl.BlockSpec((1, n_exp), lambda i: (l, 0))],
        out_specs=[pl.BlockSpec((tm, d), lambda i: (i, 0)),
                   pl.BlockSpec((tm, LANES), lambda i: (i, 0)),
                   pl.BlockSpec((tm, LANES), lambda i: (i, 0))],
        out_shape=[jax.ShapeDtypeStruct((n, d), F32),
                   jax.ShapeDtypeStruct((n, LANES), jnp.int32),
                   jax.ShapeDtypeStruct((n, LANES), F32)],
        compiler_params=_cp(("arbitrary",)),
        name="norm_router",
    )(x, g, mod3, wr, br)


def _row_copy(src, dst, sem, s_row, d_slot):
    return pltpu.make_async_copy(src.at[pl.ds(s_row, 1), :], dst.at[pl.ds(d_slot, 1), :], sem)


DMA_UNROLL = 8


def _row_gather_step(idx_ref, src_ref, buf, sem, n_rows, n_steps):
    i = pl.program_id(0)

    def issue(step, slot):
        def start(r, carry):
            _row_copy(src_ref, buf.at[slot], sem.at[slot], idx_ref[step * n_rows + r], r).start()
            return carry

        lax.fori_loop(0, n_rows, start, 0, unroll=DMA_UNROLL)

    @pl.when(i == 0)
    def _():
        issue(0, 0)

    @pl.when(i + 1 < n_steps)
    def _():
        issue(i + 1, (i + 1) % 2)

    slot = i % 2

    def wait(r, carry):
        _row_copy(src_ref, buf.at[slot], sem.at[slot], 0, r).wait()
        return carry

    lax.fori_loop(0, n_rows, wait, 0, unroll=DMA_UNROLL)
    return slot


def _gather_body(rows_ref, src_ref, o_ref, buf, sem, *, n_rows, n_steps):
    slot = _row_gather_step(rows_ref, src_ref, buf, sem, n_rows, n_steps)
    o_ref[...] = buf[slot].astype(o_ref.dtype)


def _moe_gather(rows, h2, tm):
    p_max = rows.shape[0]
    d = h2.shape[1]
    return pl.pallas_call(
        functools.partial(_gather_body, n_rows=tm, n_steps=p_max // tm),
        grid_spec=pltpu.PrefetchScalarGridSpec(
            num_scalar_prefetch=1,
            grid=(p_max // tm,),
            in_specs=[pl.BlockSpec(memory_space=pl.ANY)],
            out_specs=pl.BlockSpec((tm, d), lambda i, rows: (i, 0)),
            scratch_shapes=[pltpu.VMEM((2, tm, d), F32), pltpu.SemaphoreType.DMA((2,))]),
        out_shape=jax.ShapeDtypeStruct((p_max, d), BF16),
        compiler_params=_cp(("arbitrary",)),
        name="moe_gather",
    )(rows, h2)


def _expert_weights(plan, w_hbm, wbuf, wbf, sem, col_blocks, tn, e0, n_pass):
    blk_e, _, first, grp, e_next, meta = plan
    j = pl.program_id(0)
    i = pl.program_id(1)
    n_grp = meta[1]

    def copies(e_idx, jj):
        return [pltpu.make_async_copy(w_hbm.at[e0 + e_idx, :, pl.ds(cb * tn, tn)], wbuf.at[p], sem.at[p])
                for p, cb in enumerate(col_blocks(jj))]

    @pl.when(first[i] == 1)
    def _():
        g = grp[i]

        @pl.when((j == 0) & (g == 0))
        def _():
            for c in copies(blk_e[0], 0):
                c.start()

        for c in copies(blk_e[i], j):
            c.wait()
        for p in range(wbf.shape[0]):
            wbf[p] = wbuf[p].astype(BF16)
        last_in_pass = g == n_grp - 1

        @pl.when(jnp.logical_not(last_in_pass & (j == n_pass - 1)))
        def _():
            for c in copies(e_next[i], jnp.where(last_in_pass, j + 1, j)):
                c.start()


def _moe_gu_body(*refs, tn, e0, nj):
    plan, (x_ref, w_hbm, bg_ref, bu_ref, o_ref, wbuf, wbf, sem) = refs[:6], refs[6:]
    i = pl.program_id(1)
    nv = plan[5][0]
    _expert_weights(plan, w_hbm, wbuf, wbf, sem, lambda jj: (jj, nj + jj), tn, e0, nj)

    @pl.when(i < nv)
    def _():
        x = x_ref[...]
        g = jnp.minimum(_dot(x, wbf[0]) + bg_ref[0], SWIGLU_LIMIT)
        u = jnp.clip(_dot(x, wbf[1]) + bu_ref[0], -SWIGLU_LIMIT, SWIGLU_LIMIT)
        o_ref[...] = ((u + 1.0) * g * _sigmoid(SWIGLU_ALPHA * g)).astype(o_ref.dtype)

    @pl.when(i >= nv)
    def _():
        o_ref[...] = jnp.zeros(o_ref.shape, o_ref.dtype)


def _moe_gate_up(plan, xs, w, b, l, n_exp, tm):
    p_max, d = xs.shape
    de = w.shape[2] // 2
    tn = _pick(de, 512)
    nj = de // tn
    e0 = l * n_exp
    return pl.pallas_call(
        functools.partial(_moe_gu_body, tn=tn, e0=e0, nj=nj),
        grid_spec=pltpu.PrefetchScalarGridSpec(
            num_scalar_prefetch=6,
            grid=(nj, p_max // tm),
            in_specs=[pl.BlockSpec((tm, d), lambda j, i, be, rb, *_: (rb[i], 0)),
                      pl.BlockSpec(memory_space=pl.ANY),
                      pl.BlockSpec((1, 1, tn), lambda j, i, be, *_: (e0 + be[i], 0, j)),
                      pl.BlockSpec((1, 1, tn), lambda j, i, be, *_: (e0 + be[i], 0, nj + j))],
            out_specs=pl.BlockSpec((tm, tn), lambda j, i, *_: (i, j)),
            scratch_shapes=[pltpu.VMEM((2, d, tn), F32), pltpu.VMEM((2, d, tn), BF16),
                            pltpu.SemaphoreType.DMA((2,))]),
        out_shape=jax.ShapeDtypeStruct((p_max, de), BF16),
        compiler_params=_cp(("arbitrary", "arbitrary")),
        name="moe_gate_up",
    )(*plan, xs, w, b, b)


def _moe_down_body(*refs, tn, e0, nj):
    plan, (x_ref, w_hbm, b_ref, o_ref, wbuf, wbf, sem) = refs[:6], refs[6:]
    i = pl.program_id(1)
    nv = plan[5][0]
    _expert_weights(plan, w_hbm, wbuf, wbf, sem, lambda jj: (jj,), tn, e0, nj)

    @pl.when(i < nv)
    def _():
        o_ref[...] = _dot(x_ref[...], wbf[0]) + b_ref[0]

    @pl.when(i >= nv)
    def _():
        o_ref[...] = jnp.zeros(o_ref.shape, o_ref.dtype)


def _moe_down(plan, act, w, b, l, n_exp, tm):
    p_max, de = act.shape
    d = w.shape[2]
    tn = _pick(d, 1024)
    nj = d // tn
    e0 = l * n_exp
    return pl.pallas_call(
        functools.partial(_moe_down_body, tn=tn, e0=e0, nj=nj),
        grid_spec=pltpu.PrefetchScalarGridSpec(
            num_scalar_prefetch=6,
            grid=(nj, p_max // tm),
            in_specs=[pl.BlockSpec((tm, de), lambda j, i, be, rb, *_: (rb[i], 0)),
                      pl.BlockSpec(memory_space=pl.ANY),
                      pl.BlockSpec((1, 1, tn), lambda j, i, be, *_: (e0 + be[i], 0, j))],
            out_specs=pl.BlockSpec((tm, tn), lambda j, i, *_: (i, j)),
            scratch_shapes=[pltpu.VMEM((1, de, tn), F32), pltpu.VMEM((1, de, tn), BF16),
                            pltpu.SemaphoreType.DMA((1,))]),
        out_shape=jax.ShapeDtypeStruct((p_max, d), F32),
        compiler_params=_cp(("arbitrary", "arbitrary")),
        name="moe_down",
    )(*plan, act, w, b)


def _combine_body(dest_ref, yb_ref, x_ref, p_ref, mod_ref, op_ref, os_ref, buf, sem, *, geom, n_steps):
    i = pl.program_id(0)
    slot = _row_gather_step(dest_ref, yb_ref, buf, sem, ROW_BLOCK * TOP_K, n_steps)
    b = geom.bidx(i)
    p = p_ref[...]
    acc = p[:, 0:1] * buf[slot, pl.ds(0, ROW_BLOCK), :]
    for k in range(1, TOP_K):
        acc = acc + p[:, k:k + 1] * buf[slot, pl.ds(k * ROW_BLOCK, ROW_BLOCK), :]
    val = x_ref[...] + mod_ref[b, pl.ds(5, 1), :] * acc
    n_pb = geom.n_p // ROW_BLOCK

    @pl.when(i < n_pb)
    def _():
        op_ref[...] = val

    @pl.when(i >= n_pb)
    def _():
        os_ref[...] = val


def _moe_combine(dest_km, yb, x1, probs, mod3, geom):
    n, d = x1.shape
    n_pb = geom.n_p // ROW_BLOCK
    return pl.pallas_call(
        functools.partial(_combine_body, geom=geom, n_steps=n // ROW_BLOCK),
        grid_spec=pltpu.PrefetchScalarGridSpec(
            num_scalar_prefetch=1,
            grid=(n // ROW_BLOCK,),
            in_specs=[pl.BlockSpec(memory_space=pl.ANY),
                      pl.BlockSpec((ROW_BLOCK, d), lambda i, dest: (i, 0)),
                      pl.BlockSpec((ROW_BLOCK, LANES), lambda i, dest: (i, 0)),
                      pl.BlockSpec(mod3.shape, lambda i, dest: (0, 0, 0))],
            out_specs=[pl.BlockSpec((ROW_BLOCK, d), lambda i, dest: (jnp.minimum(i, n_pb - 1), 0)),
                       pl.BlockSpec((ROW_BLOCK, d), lambda i, dest: (jnp.maximum(i - n_pb, 0), 0))],
            scratch_shapes=[pltpu.VMEM((2, ROW_BLOCK * TOP_K, d), F32), pltpu.SemaphoreType.DMA((2,))]),
        out_shape=[jax.ShapeDtypeStruct((geom.n_p, d), F32), jax.ShapeDtypeStruct((geom.n_s, d), F32)],
        compiler_params=_cp(("arbitrary",)),
        name="moe_combine",
    )(dest_km, yb, x1, probs, mod3)


def _moe_plan(top_idx, n_exp, tm):
    n_tok = top_idx.shape[0]
    n_slot = n_tok * TOP_K
    flat_e = top_idx.reshape(n_slot)
    onehot = (flat_e[:, None] == jnp.arange(n_exp, dtype=jnp.int32)[None, :]).astype(jnp.int32)
    csum = jnp.cumsum(onehot, axis=0)
    rank = jnp.sum(onehot * csum, axis=1) - 1
    counts = csum[-1]
    p_counts = (counts + tm - 1) // tm * tm
    p_ends = jnp.cumsum(p_counts)
    p_starts = p_ends - p_counts
    dest = (p_starts[flat_e] + rank).astype(jnp.int32)
    n_tiles = n_slot // tm + n_exp
    p_max = n_tiles * tm
    flat_tok = jnp.arange(n_slot, dtype=jnp.int32) // TOP_K
    rows = jnp.zeros((p_max,), jnp.int32).at[dest].set(flat_tok, unique_indices=True)
    tile_start = jnp.arange(n_tiles, dtype=jnp.int32) * tm
    nv = (p_ends[-1] // tm).astype(jnp.int32)
    blk_e = jnp.minimum(jnp.searchsorted(p_ends, tile_start, side="right"), n_exp - 1).astype(jnp.int32)
    last_e = blk_e[jnp.maximum(nv - 1, 0)]
    tile_id = jnp.arange(n_tiles, dtype=jnp.int32)
    blk_e = jnp.where(tile_id < nv, blk_e, last_e)
    row_blk = jnp.minimum(tile_id, jnp.maximum(nv - 1, 0))
    first = (tile_id < nv) & ((tile_id == 0) | (blk_e != jnp.roll(blk_e, 1)))
    grp = jnp.cumsum(first.astype(jnp.int32)) - 1
    n_grp = jnp.sum(first.astype(jnp.int32))
    starts = jnp.sort(jnp.where(first, tile_id, n_tiles))
    grp_e = blk_e[jnp.minimum(starts, n_tiles - 1)]
    e_next = grp_e[(grp + 1) % n_grp]
    meta = jnp.stack([nv, n_grp]).astype(jnp.int32)
    plan = (blk_e, row_blk, first.astype(jnp.int32), grp.astype(jnp.int32), e_next.astype(jnp.int32), meta)
    dest_km = dest.reshape(n_tok // ROW_BLOCK, ROW_BLOCK, TOP_K).transpose(0, 2, 1).reshape(n_slot)
    return rows, dest_km, plan


def kernel(x_prompt, x_sample, cache_k, cache_v, state_ssm_re, state_ssm_im, c_prompt, c_sample,
           w_ada, b_ada, g_norm_mix, g_norm_ffn, w_in, b_in, g_q, g_k, w_attn_up,
           ssm_a_re, ssm_a_im, ssm_log_dt, ssm_b_re, ssm_b_im, ssm_c_re, ssm_c_im, ssm_d,
           w_glu, w_out, w_router, b_router, w_gate_up, b_gate_up, w_down, b_down):
    B, T, D = x_prompt.shape
    DB, DT, _ = x_sample.shape
    depth = w_ada.shape[0]
    H, hd = cache_k.shape[3], cache_k.shape[4]
    AW = H * hd
    G, N, P = ssm_b_re.shape[1:]
    SW = G * P
    E = w_router.shape[2]
    assert hd == LANES and P == 16 and 2 * N == LANES and DT == ROW_BLOCK
    geom = _Geom(B, T, DB, DT)
    n_p, n_tok, nbp = geom.n_p, geom.n_tok, geom.nbp
    L = SSM_CHUNK
    assert T % L == 0 and DT % L == 0
    moe_tm = 256

    xp, xs_ = x_prompt.reshape(n_p, D), x_sample.reshape(geom.n_s, D)
    c_all = jnp.concatenate([c_prompt, c_sample, jnp.zeros((nbp - geom.nb, D), F32)], axis=0)

    flat2 = lambda w: w.reshape((w.shape[0] * w.shape[1],) + w.shape[2:])
    w_ada2, w_in2, w_au2, w_glu2, w_out2, w_r2 = map(flat2, (w_ada, w_in, w_attn_up, w_glu, w_out, w_router))
    w_gu3, w_dn3 = flat2(w_gate_up), flat2(w_down)
    b_gu3 = b_gate_up.reshape(depth * E, 1, b_gate_up.shape[2])
    b_dn3 = b_down.reshape(depth * E, 1, b_down.shape[2])

    NG = LANES // P
    assert G % NG == 0
    GB, NS = G // NG, NG * N
    lam = jnp.stack([ssm_a_re.reshape(depth * GB, NS), ssm_a_im.reshape(depth * GB, NS)], axis=1)
    ldt = jnp.repeat(ssm_log_dt, N, axis=-1).reshape(depth * GB, 1, NS)
    bb = jnp.stack([ssm_b_re.transpose(0, 1, 3, 2).reshape(depth * GB, LANES, N),
                    ssm_b_im.transpose(0, 1, 3, 2).reshape(depth * GB, LANES, N)], axis=1)
    cb = jnp.stack([ssm_c_re.reshape(depth * GB, LANES, N), ssm_c_im.reshape(depth * GB, LANES, N)], axis=1)
    db = ssm_d.reshape(depth * GB, 1, LANES)

    nc_p, nc_s = n_p // L, geom.n_s // L
    NC = nc_p + nc_s
    cps, css = T // L, DT // L
    lidx = jnp.concatenate([jnp.arange(nc_p, dtype=jnp.int32) % cps,
                            jnp.arange(nc_s, dtype=jnp.int32) % css]).reshape(NC, 1)
    n_steps = max(cps - 1, css - 1, 1).bit_length()
    last_rows = tuple((b + 1) * cps - 1 for b in range(B)) + tuple(nc_p + (b + 1) * css - 1 for b in range(DB))
    first_rows = tuple(nc_p + b * css for b in range(DB))

    ks, vs, srs, sis = [], [], [], []
    for l in range(depth):
        mod = _ada(c_all, w_ada2, b_ada, l)
        mod3 = mod.reshape(nbp, 6, D)
        h = _norm_mod(xp, xs_, g_norm_mix, mod3, geom, l, 0, 1)

        dn = functools.partial(_dense, h, w_in2, l, geom=geom, bias=b_in)
        split_f32 = ((F32, "p"), (F32, "s"), (BF16, "all"))
        (qb,) = dn(0, AW, mode="q", extra=(g_q,), extra_kinds=("row",), out_kinds=((BF16, "all"),), name="proj_q")
        kfp, kfs, kb = dn(AW, AW, mode="k", extra=(g_k,), extra_kinds=("row",), out_kinds=split_f32, name="proj_k")
        vfp, vfs, vb = dn(2 * AW, AW, mode="v", out_kinds=split_f32, name="proj_v")
        (u,) = dn(3 * AW, SW, mode="plain", name="proj_u")
        (sga,) = dn(3 * AW + SW, D, mode="sig", name="proj_ga")
        (sgb,) = dn(3 * AW + SW + D, D, mode="sig", name="proj_gb")

        attn = jnp.concatenate([_sb_prompt(qb, kb, vb, B, T, H, hd),
                                _sb_sample(qb, kb, vb, cache_k, cache_v, l, DB, DT, H, hd, n_p)], axis=0)
        (ma,) = _dense(attn, w_au2, l, 0, D, mode="mul", geom=geom, extra=(sga,), extra_kinds=("tile",),
                       name="attn_up")

        x0 = jnp.concatenate([state_ssm_re[l].reshape(DB, GB, NS), state_ssm_im[l].reshape(DB, GB, NS)],
                             axis=-1).transpose(1, 0, 2)
        yg, fin = _s5(u, x0, lidx, lam, ldt, bb, cb, db, l, N, n_steps, first_rows, last_rows)
        fin = fin.reshape(GB, geom.nb, 2, NG, N).transpose(2, 1, 0, 3, 4).reshape(2, geom.nb, G, N)

        merged = _glu_merge(yg, w_glu2, l, ma, sgb, geom)
        (x1,) = _dense(merged, w_out2, l, 0, D, mode="resid", geom=geom, extra=(xp, xs_, mod3),
                       extra_kinds=("tile_p", "tile_s", "mod"), name="out_proj")

        h2, top_idx, probs = _norm_router(x1, g_norm_ffn, mod3, w_r2, b_router, geom, l)
        rows, dest_km, plan = _moe_plan(top_idx[:, :TOP_K], E, moe_tm)
        xs = _moe_gather(rows, h2, moe_tm)
        act = _moe_gate_up(plan, xs, w_gu3, b_gu3, l, E, moe_tm)
        yb = _moe_down(plan, act, w_dn3, b_dn3, l, E, moe_tm)
        xp, xs_ = _moe_combine(dest_km, yb, x1, probs, mod3, geom)

        ks.append((kfp, kfs))
        vs.append((vfp, vfs))
        srs.append(fin[0])
        sis.append(fin[1])

    def stacked(pairs, part, lead):
        return jnp.stack([p[part] for p in pairs]).reshape((depth,) + lead + (H, hd))

    sr, si = jnp.stack(srs), jnp.stack(sis)
    return (xp.reshape(B, T, D), xs_.reshape(DB, DT, D), stacked(ks, 0, (B, T)), stacked(vs, 0, (B, T)),
            sr[:, :B], si[:, :B], stacked(ks, 1, (DB, DT)), stacked(vs, 1, (DB, DT)), sr[:, B:], si[:, B:])
```

```python
import functools
import math

import jax
import jax.numpy as jnp
from jax import lax
from jax.experimental import pallas as pl
from jax.experimental.pallas import tpu as pltpu

F32 = jnp.float32
BF16 = jnp.bfloat16

NORM_EPS = 1e-6
TOP_K = 4
SWIGLU_LIMIT = 7.0
SWIGLU_ALPHA = 1.702
ROW_BLOCK = 64
SSM_CHUNK = 16
LANES = 128
VMEM_LIMIT = 60 * 1024 * 1024


def _cp(sem, vmem=VMEM_LIMIT):
    return pltpu.CompilerParams(dimension_semantics=sem, vmem_limit_bytes=vmem)


def _pick(n, pref):
    t = pref
    while t > 1 and n % t:
        t //= 2
    return t


def _sigmoid(x):
    return 1.0 / (1.0 + jnp.exp(-x))


def _dot(a, b):
    return jnp.dot(a, b, preferred_element_type=F32)


def _split(a):
    hi = a.astype(BF16)
    lo = (a - hi.astype(F32)).astype(BF16)
    return hi, lo


def _dot3(a, b):
    ah, al = _split(a)
    bh, bl = _split(b)
    return _dot(ah, bh) + (_dot(al, bh) + _dot(ah, bl))


class _Geom:
    def __init__(self, B, T, DB, DT):
        self.B, self.T, self.DB, self.DT = B, T, DB, DT
        self.n_p, self.n_s = B * T, DB * DT
        self.n_tok = self.n_p + self.n_s
        assert T % ROW_BLOCK == 0 and DT % ROW_BLOCK == 0
        self.tm = 512
        while self.n_p % self.tm or self.n_s % self.tm:
            self.tm //= 2
        assert self.tm >= ROW_BLOCK
        self.sub = self.tm // ROW_BLOCK
        self.npt = self.n_p // self.tm
        self.nb = B + DB
        self.nbp = -(-self.nb // 8) * 8

    def p_blk(self, i):
        return jnp.minimum(i, self.npt - 1)

    def s_blk(self, i):
        return jnp.maximum(i - self.npt, 0)

    def bidx(self, blk):
        pb = self.n_p // ROW_BLOCK
        return jnp.where(blk < pb, blk // (self.T // ROW_BLOCK),
                         self.B + (blk - pb) // (self.DT // ROW_BLOCK))


def _ada_body(c_ref, w_ref, b_ref, o_ref):
    c = c_ref[...]
    s = c * _sigmoid(c)
    o_ref[...] = _dot(s.astype(BF16), w_ref[...].astype(BF16)) + b_ref[...]


def _ada(c_pad, w, b, l):
    nbp, d = c_pad.shape
    n = w.shape[1]
    tn = _pick(n, 1024)
    return pl.pallas_call(
        _ada_body,
        grid=(n // tn,),
        in_specs=[pl.BlockSpec((nbp, d), lambda j: (0, 0)),
                  pl.BlockSpec((d, tn), lambda j: (l, j)),
                  pl.BlockSpec((1, tn), lambda j: (l, j))],
        out_specs=pl.BlockSpec((nbp, tn), lambda j: (0, j)),
        out_shape=jax.ShapeDtypeStruct((nbp, n), F32),
        compiler_params=_cp(("arbitrary",)),
        name="ada_mod",
    )(c_pad, w, b)


def _norm_rows(x, g, sc, sh):
    ms = jnp.mean(x * x, axis=-1, keepdims=True)
    return (x * lax.rsqrt(ms + NORM_EPS)) * g * (1.0 + sc) + sh


def _load_rows(i, geom, ref_p, ref_s, rows):
    return jnp.where(i < geom.npt, ref_p[rows, :], ref_s[rows, :])


def _store_tile(i, geom, ref_p, ref_s, val):
    @pl.when(i < geom.npt)
    def _():
        ref_p[...] = val

    @pl.when(i >= geom.npt)
    def _():
        ref_s[...] = val


def _norm_mod_body(xp_ref, xs_ref, g_ref, mod_ref, o_ref, *, geom, i_sh, i_sc):
    i = pl.program_id(0)
    for r in range(geom.sub):
        b = geom.bidx(i * geom.sub + r)
        rows = pl.ds(r * ROW_BLOCK, ROW_BLOCK)
        h = _norm_rows(_load_rows(i, geom, xp_ref, xs_ref, rows), g_ref[...], mod_ref[b, pl.ds(i_sc, 1), :],
                       mod_ref[b, pl.ds(i_sh, 1), :])
        o_ref[rows, :] = h.astype(o_ref.dtype)


def _norm_mod(xp, xs, g, mod3, geom, l, i_sh, i_sc):
    d = xp.shape[1]
    tm = geom.tm
    return pl.pallas_call(
        functools.partial(_norm_mod_body, geom=geom, i_sh=i_sh, i_sc=i_sc),
        grid=(geom.n_tok // tm,),
        in_specs=[pl.BlockSpec((tm, d), lambda i: (geom.p_blk(i), 0)),
                  pl.BlockSpec((tm, d), lambda i: (geom.s_blk(i), 0)),
                  pl.BlockSpec((1, d), lambda i: (l, 0)),
                  pl.BlockSpec(mod3.shape, lambda i: (0, 0, 0))],
        out_specs=pl.BlockSpec((tm, d), lambda i: (i, 0)),
        out_shape=jax.ShapeDtypeStruct((geom.n_tok, d), BF16),
        compiler_params=_cp(("arbitrary",)),
        name="norm_mod",
    )(xp, xs, g, mod3)


def _head_norm(acc, g):
    outs = []
    for c in range(acc.shape[1] // LANES):
        a = acc[:, c * LANES:(c + 1) * LANES]
        ms = jnp.mean(a * a, axis=-1, keepdims=True)
        outs.append(a * lax.rsqrt(ms + NORM_EPS) * g)
    return jnp.concatenate(outs, axis=1)


def _dense_body(*refs, mode, geom, n_extra, n_out, has_bias):
    x_ref, w_ref = refs[0], refs[1]
    pos = 2
    b_ref = None
    if has_bias:
        b_ref = refs[pos]
        pos += 1
    extra = refs[pos:pos + n_extra]
    outs = refs[pos + n_extra:pos + n_extra + n_out]
    wb = refs[pos + n_extra + n_out]
    i = pl.program_id(1)

    @pl.when(i == 0)
    def _():
        wb[...] = w_ref[...].astype(BF16)

    acc = _dot(x_ref[...], wb[...])
    if has_bias:
        acc = acc + b_ref[...]
    if mode == "q":
        qscale = LANES ** -0.5 * LOG2_E
        outs[0][...] = (_head_norm(acc, extra[0][...]) * qscale).astype(BF16)
    elif mode == "k":
        a = _head_norm(acc, extra[0][...])
        _store_tile(i, geom, outs[0], outs[1], a)
        outs[2][...] = a.astype(BF16)
    elif mode == "v":
        _store_tile(i, geom, outs[0], outs[1], acc)
        outs[2][...] = acc.astype(BF16)
    elif mode == "plain":
        outs[0][...] = acc
    elif mode == "sig":
        outs[0][...] = _sigmoid(acc)
    elif mode == "mul":
        outs[0][...] = acc * extra[0][...]
    elif mode == "resid":
        xres_p, xres_s, mod_ref = extra
        for r in range(geom.sub):
            b = geom.bidx(i * geom.sub + r)
            rows = pl.ds(r * ROW_BLOCK, ROW_BLOCK)
            gate = mod_ref[b, pl.ds(2, 1), :]
            outs[0][rows, :] = (_load_rows(i, geom, xres_p, xres_s, rows)
                                + gate * acc[r * ROW_BLOCK:(r + 1) * ROW_BLOCK, :])
    else:
        raise ValueError(mode)


def _dense(x, w, l, col0, ncols, *, mode, geom, bias=None, extra=(), extra_kinds=(), out_kinds=((F32, "all"),),
           tn_pref=1024, name="dense"):
    m, k = x.shape
    tm = geom.tm
    tn = _pick(math.gcd(ncols, col0) if col0 else ncols, tn_pref)
    cb = col0 // tn
    row_maps = {"all": lambda j, i: (i, j), "p": lambda j, i: (geom.p_blk(i), j),
                "s": lambda j, i: (geom.s_blk(i), j)}
    n_rows = {"all": m, "p": geom.n_p, "s": geom.n_s}
    in_specs = [pl.BlockSpec((tm, k), lambda j, i: (i, 0)),
                pl.BlockSpec((k, tn), lambda j, i: (l, cb + j), pipeline_mode=pl.Buffered(1))]
    args = [x, w]
    if bias is not None:
        in_specs.append(pl.BlockSpec((1, tn), lambda j, i: (l, cb + j)))
        args.append(bias)
    for a, kind in zip(extra, extra_kinds):
        if kind == "tile":
            in_specs.append(pl.BlockSpec((tm, tn), row_maps["all"]))
        elif kind in ("tile_p", "tile_s"):
            in_specs.append(pl.BlockSpec((tm, tn), row_maps[kind[-1]]))
        elif kind == "row":
            in_specs.append(pl.BlockSpec((1, a.shape[1]), lambda j, i: (l, 0)))
        elif kind == "mod":
            in_specs.append(pl.BlockSpec((a.shape[0], a.shape[1], tn), lambda j, i: (0, 0, j)))
        args.append(a)
    out_specs = [pl.BlockSpec((tm, tn), row_maps[part]) for _, part in out_kinds]
    out_shape = [jax.ShapeDtypeStruct((n_rows[part], ncols), dt) for dt, part in out_kinds]
    res = pl.pallas_call(
        functools.partial(_dense_body, mode=mode, geom=geom, n_extra=len(extra), n_out=len(out_kinds),
                          has_bias=bias is not None),
        grid=(ncols // tn, m // tm),
        in_specs=in_specs,
        out_specs=out_specs,
        out_shape=out_shape,
        scratch_shapes=[pltpu.VMEM((k, tn), BF16)],
        compiler_params=_cp(("arbitrary", "arbitrary")),
        name=name,
    )(*args)
    return res


def _glu_body(x_ref, wv_ref, wg_ref, ma_ref, sgb_ref, o_ref, wvb, wgb):
    @pl.when(pl.program_id(1) == 0)
    def _():
        wvb[...] = wv_ref[...].astype(BF16)
        wgb[...] = wg_ref[...].astype(BF16)

    x = x_ref[...].astype(BF16)
    val = _dot(x, wvb[...])
    gate = _dot(x, wgb[...])
    o_ref[...] = (ma_ref[...] + sgb_ref[...] * (val * _sigmoid(gate))).astype(o_ref.dtype)


def _glu_merge(yg, w, l, ma, sgb, geom):
    m, k = yg.shape
    d = ma.shape[1]
    tm = geom.tm
    tn = _pick(d, 1024)
    nj = d // tn
    return pl.pallas_call(
        _glu_body,
        grid=(nj, m // tm),
        in_specs=[pl.BlockSpec((tm, k), lambda j, i: (i, 0)),
                  pl.BlockSpec((k, tn), lambda j, i: (l, j), pipeline_mode=pl.Buffered(1)),
                  pl.BlockSpec((k, tn), lambda j, i: (l, nj + j), pipeline_mode=pl.Buffered(1)),
                  pl.BlockSpec((tm, tn), lambda j, i: (i, j)),
                  pl.BlockSpec((tm, tn), lambda j, i: (i, j))],
        out_specs=pl.BlockSpec((tm, tn), lambda j, i: (i, j)),
        out_shape=jax.ShapeDtypeStruct((m, d), BF16),
        scratch_shapes=[pltpu.VMEM((k, tn), BF16), pltpu.VMEM((k, tn), BF16)],
        compiler_params=_cp(("arbitrary", "arbitrary")),
        name="glu_merge",
    )(yg, w, w, ma, sgb)


LOG2_E = 1.4426950408889634


def _neg_abs(z):
    return lax.bitcast_convert_type(lax.bitcast_convert_type(z, jnp.uint32) | jnp.uint32(0x80000000), F32)


def _sb_tile(q, kt, vt, tri, mask, c, o):
    z = lax.dot_general(q, kt, (((1,), (1,)), ((), ())), preferred_element_type=F32)
    sp = jnp.maximum(z, 0.0) + jnp.log2(1.0 + jnp.exp2(_neg_abs(z)))
    if mask is not None:
        sp = jnp.where(mask, sp, 0.0)
    incl = _dot(sp.astype(BF16), tri)
    w = jnp.exp2(z - incl - c)
    if mask is not None:
        w = jnp.where(mask, w, 0.0)
    o = o + _dot(w.astype(BF16), vt)
    c = c + incl[:, 0:1]
    return c, o


def _suffix_matrix(tk):
    j = lax.broadcasted_iota(jnp.int32, (tk, tk), 0)
    s = lax.broadcasted_iota(jnp.int32, (tk, tk), 1)
    return (j >= s).astype(BF16)


def _sb_prompt_body(q_ref, k_ref, v_ref, o_ref, *, tq, tk):
    qi = pl.program_id(2)
    q = q_ref[...]
    r = tq // tk
    tri = _suffix_matrix(tk)
    row = lax.broadcasted_iota(jnp.int32, (tq, tk), 0)
    col = lax.broadcasted_iota(jnp.int32, (tq, tk), 1)
    c = jnp.zeros((tq, 1), F32)
    o = jnp.zeros((tq, q.shape[1]), F32)
    for d in reversed(range(r)):
        ks = pl.ds(pl.multiple_of((qi * r + d) * tk, tk), tk)
        c, o = _sb_tile(q, k_ref[ks, :], v_ref[ks, :], tri, col + d * tk < row, c, o)

    def body(it, carry):
        c, o = carry
        for u in range(r):
            ks = pl.ds(pl.multiple_of((qi * r - 1 - (it * r + u)) * tk, tk), tk)
            c, o = _sb_tile(q, k_ref[ks, :], v_ref[ks, :], tri, None, c, o)
        return c, o

    c, o = lax.fori_loop(0, qi, body, (c, o))
    o_ref[...] = o.astype(o_ref.dtype)


def _sb_prompt(q, kb, vb, B, T, H, hd):
    tq = _pick(T, 512)
    tk = _pick(tq, 256)
    nq = T // tq
    return pl.pallas_call(
        functools.partial(_sb_prompt_body, tq=tq, tk=tk),
        grid=(B, H, nq),
        in_specs=[pl.BlockSpec((tq, hd), lambda b, h, i: (b * nq + i, h)),
                  pl.BlockSpec((T, hd), lambda b, h, i: (b, h)),
                  pl.BlockSpec((T, hd), lambda b, h, i: (b, h))],
        out_specs=pl.BlockSpec((tq, hd), lambda b, h, i: (b * nq + i, h)),
        out_shape=jax.ShapeDtypeStruct((B * T, H * hd), BF16),
        compiler_params=_cp(("arbitrary", "arbitrary", "arbitrary")),
        name="sb_prompt",
    )(q, kb, vb)


def _sb_sample_body(q_ref, k_ref, v_ref, ck_ref, cv_ref, o_ref, *, dt, tk, past, hb, hd):
    row = lax.broadcasted_iota(jnp.int32, (dt, dt), 0)
    col = lax.broadcasted_iota(jnp.int32, (dt, dt), 1)
    tri_new = _suffix_matrix(dt)
    tri = _suffix_matrix(tk)
    n_t = past // tk
    heads = [slice(h * hd, (h + 1) * hd) for h in range(hb)]
    c0 = jnp.zeros((dt, 1), F32)
    o0 = jnp.zeros((dt, hd), F32)
    carry = tuple(_sb_tile(q_ref[:, sl], k_ref[:, sl], v_ref[:, sl], tri_new, col < row, c0, o0)
                  for sl in heads)

    def body(it, carry):
        ks = pl.ds(pl.multiple_of((n_t - 1 - it) * tk, tk), tk)
        return tuple(_sb_tile(q_ref[:, sl], ck_ref[0, ks, sl].astype(BF16), cv_ref[0, ks, sl].astype(BF16),
                              tri, None, co[0], co[1])
                     for sl, co in zip(heads, carry))

    carry = lax.fori_loop(0, n_t, body, carry)
    for sl, co in zip(heads, carry):
        o_ref[:, sl] = co[1].astype(o_ref.dtype)


def _sb_sample(q, kb, vb, ck, cv, l, DB, DT, H, hd, n_p):
    past = ck.shape[2]
    tk = _pick(past, 256)
    hb = _pick(H, 4)
    rb0 = n_p // DT
    ck3 = ck.reshape(ck.shape[0] * DB, past, H * hd)
    cv3 = cv.reshape(cv.shape[0] * DB, past, H * hd)
    new = lambda b, h: (rb0 + b, h)
    return pl.pallas_call(
        functools.partial(_sb_sample_body, dt=DT, tk=tk, past=past, hb=hb, hd=hd),
        grid=(DB, H // hb),
        in_specs=[pl.BlockSpec((DT, hb * hd), new),
                  pl.BlockSpec((DT, hb * hd), new),
                  pl.BlockSpec((DT, hb * hd), new),
                  pl.BlockSpec((1, past, hb * hd), lambda b, h: (l * DB + b, 0, h)),
                  pl.BlockSpec((1, past, hb * hd), lambda b, h: (l * DB + b, 0, h))],
        out_specs=pl.BlockSpec((DT, hb * hd), lambda b, h: (b, h)),
        out_shape=jax.ShapeDtypeStruct((DB * DT, H * hd), BF16),
        compiler_params=_cp(("arbitrary", "arbitrary")),
        name="sb_sample",
    )(q, kb, vb, ck3, cv3)


def _cexp(lr, li, t):
    mag = jnp.exp(lr * t)
    return mag * jnp.cos(li * t), mag * jnp.sin(li * t)


def _split3(a):
    hi = a.astype(BF16)
    r1 = a - hi.astype(F32)
    mid = r1.astype(BF16)
    return hi, mid, (r1 - mid.astype(F32)).astype(BF16)


NT_DIMS = (((1,), (1,)), ((), ()))


def _s5_body(u_ref, x0_ref, lidx_ref, lam_ref, ldt_ref, b_ref, c_ref, d_ref, y_ref, fin_ref,
             wst, vst, tbig, ucat, x0s, e_scr, *, n_state, n_steps, first_rows, last_rows):
    L, P, N = SSM_CHUNK, 16, n_state
    NS = (LANES // P) * N
    nc = ucat.shape[0]
    lr, li = lam_ref[0, 0:1, :], lam_ref[0, 1:2, :]
    dt = jnp.exp(ldt_ref[0])
    abr, abi = _cexp(lr, li, dt)
    den = lr * lr + li * li
    nr = abr - 1.0
    cor = (nr * lr + abi * li) / den
    coi = (abi * lr - nr * li) / den

    expand = (lax.broadcasted_iota(jnp.int32, (N, NS), 1) % N
              == lax.broadcasted_iota(jnp.int32, (N, NS), 0)).astype(BF16)
    own = (lax.broadcasted_iota(jnp.int32, (LANES, NS), 0) // P
           == lax.broadcasted_iota(jnp.int32, (LANES, NS), 1) // N)

    def block_diag(x):
        h, m, lo = _split3(x)
        return jnp.where(own, _dot(h, expand) + (_dot(m, expand) + _dot(lo, expand)), 0.0)

    spread = (lax.broadcasted_iota(jnp.int32, (LANES, P), 0) % P
              == lax.broadcasted_iota(jnp.int32, (LANES, P), 1)).astype(BF16)

    def block_diag_t(x):
        h, m, lo = _split3(x)
        nt = lambda a: lax.dot_general(spread, a, NT_DIMS, preferred_element_type=F32)
        return jnp.where(own, nt(h) + (nt(m) + nt(lo)), 0.0)

    br, bi = block_diag_t(b_ref[0, 0]), block_diag_t(b_ref[0, 1])
    cr, ci = block_diag(c_ref[0, 0]), block_diag(c_ref[0, 1])
    wr, wi = cor * br - coi * bi, cor * bi + coi * br
    vr, vi = cr * abr - ci * abi, cr * abi + ci * abr
    for m in range(L):
        wst[pl.ds((L - 1 - m) * LANES, LANES), :] = jnp.concatenate([wr, wi], axis=1).astype(BF16)
        vst[pl.ds(m * LANES, LANES), :] = jnp.concatenate([vr, -vi], axis=1).astype(BF16)
        if m + 1 < L:
            wr, wi = wr * abr - wi * abi, wr * abi + wi * abr
            vr, vi = vr * abr - vi * abi, vr * abi + vi * abr

    cc = jnp.concatenate([cr, -ci], axis=1).astype(BF16)
    t_all = lax.dot_general(wst[...], cc, NT_DIMS, preferred_element_type=F32)
    eye = (lax.broadcasted_iota(jnp.int32, (LANES, LANES), 0)
           == lax.broadcasted_iota(jnp.int32, (LANES, LANES), 1))
    t0 = t_all[(L - 1) * LANES:, :] + jnp.where(eye, d_ref[0], 0.0)
    t_bf = jnp.concatenate([t_all[:(L - 1) * LANES, :], t0], axis=0).astype(BF16)
    for i in range(L):
        cols = pl.ds(i * LANES, LANES)
        tbig[pl.ds(0, (i + 1) * LANES), cols] = t_bf[(L - 1 - i) * LANES:, :]
        if i + 1 < L:
            tbig[pl.ds((i + 1) * LANES, (L - 1 - i) * LANES), cols] = jnp.zeros(((L - 1 - i) * LANES, LANES), BF16)

    for j in range(L):
        ucat[:, pl.ds(j * LANES, LANES)] = u_ref[pl.ds(j, nc, stride=L), :].astype(BF16)
    uc = ucat[...]
    s_loc = _dot(uc, wst[...])

    x0s[...] = jnp.zeros(x0s.shape, F32)
    for b, row in enumerate(first_rows):
        x0s[pl.ds(row, 1), :] = x0_ref[0, pl.ds(b, 1), :]
    x0 = x0s[...]

    def cmul(t, s):
        pr, pi = _cexp(lr, li, t * dt)
        sr, si = s[:, :NS], s[:, NS:]
        return jnp.concatenate([pr * sr - pi * si, pr * si + pi * sr], axis=1)

    lidx = lidx_ref[...]
    e = s_loc + cmul(float(L), x0)
    for kk in range(n_steps):
        sh = pltpu.roll(e, 1 << kk, axis=0)
        e = e + jnp.where(lidx >= (1 << kk), cmul(float(L * (1 << kk)), sh), 0.0)
    s_in = jnp.where(lidx >= 1, pltpu.roll(e, 1, axis=0), x0)
    y = _dot(uc, tbig[...]) + lax.dot_general(s_in.astype(BF16), vst[...], NT_DIMS, preferred_element_type=F32)
    g = 0.5 * y * (1.0 + lax.erf(y * (2.0 ** -0.5)))
    for i in range(L):
        y_ref[pl.ds(i, nc, stride=L), :] = g[:, i * LANES:(i + 1) * LANES]
    e_scr[...] = e
    for r, src in enumerate(last_rows):
        fin_ref[0, pl.ds(r, 1), :] = e_scr[pl.ds(src, 1), :]


def _s5(u, x0, lidx, lam, ldt, bb, cb, db, l, n_state, n_steps, first_rows, last_rows):
    n_tok, sw = u.shape
    gb = sw // LANES
    nc = n_tok // SSM_CHUNK
    ns2 = 2 * (LANES // 16) * n_state
    nb = len(last_rows)
    w = SSM_CHUNK * LANES
    blk3 = lambda g: (l * gb + g, 0, 0)
    blk4 = lambda g: (l * gb + g, 0, 0, 0)
    return pl.pallas_call(
        functools.partial(_s5_body, n_state=n_state, n_steps=n_steps, first_rows=first_rows, last_rows=last_rows),
        grid=(gb,),
        in_specs=[pl.BlockSpec((n_tok, LANES), lambda g: (0, g)),
                  pl.BlockSpec((1, x0.shape[1], ns2), lambda g: (g, 0, 0)),
                  pl.BlockSpec((nc, 1), lambda g: (0, 0)),
                  pl.BlockSpec((1, 2, ns2 // 2), blk3),
                  pl.BlockSpec((1, 1, ns2 // 2), blk3),
                  pl.BlockSpec((1, 2, ns2 // 2, 16), blk4),
                  pl.BlockSpec((1, 2, LANES, n_state), blk4),
                  pl.BlockSpec((1, 1, LANES), blk3)],
        out_specs=[pl.BlockSpec((n_tok, LANES), lambda g: (0, g)),
                   pl.BlockSpec((1, nb, ns2), lambda g: (g, 0, 0))],
        out_shape=[jax.ShapeDtypeStruct((n_tok, sw), F32),
                   jax.ShapeDtypeStruct((gb, nb, ns2), F32)],
        scratch_shapes=[pltpu.VMEM((w, ns2), BF16), pltpu.VMEM((w, ns2), BF16), pltpu.VMEM((w, w), BF16),
                        pltpu.VMEM((nc, w), BF16), pltpu.VMEM((nc, ns2), F32), pltpu.VMEM((nc, ns2), F32)],
        compiler_params=_cp(("arbitrary",)),
        name="s5_block",
    )(u, x0, lidx, lam, ldt, bb, cb, db)


RANK_BLOCK = 128


def _norm_router_body(x_ref, g_ref, mod_ref, wr_ref, br_ref, h_ref, idx_ref, p_ref, rank_ref, cnt_ref, base_scr,
                      *, geom, n_exp):
    i = pl.program_id(0)
    for r in range(geom.sub):
        b = geom.bidx(i * geom.sub + r)
        rows = pl.ds(r * ROW_BLOCK, ROW_BLOCK)
        h_ref[rows, :] = _norm_rows(x_ref[rows, :], g_ref[...], mod_ref[b, pl.ds(4, 1), :],
                                    mod_ref[b, pl.ds(3, 1), :])
    logits = _dot3(h_ref[...], wr_ref[...]) + br_ref[...]
    tm = logits.shape[0]
    lane = lax.broadcasted_iota(jnp.int32, (tm, n_exp), 1)
    wide = lax.broadcasted_iota(jnp.int32, (tm, LANES), 1)
    idx_out = jnp.zeros((tm, LANES), jnp.int32)
    vals, sels = [], []
    for k in range(TOP_K):
        m = jnp.max(logits, axis=-1, keepdims=True)
        sel = jnp.min(jnp.where(logits == m, lane, n_exp), axis=-1, keepdims=True)
        idx_out = jnp.where(wide == k, sel, idx_out)
        vals.append(m)
        sels.append(sel)
        logits = jnp.where(lane == sel, -jnp.inf, logits)
    es = [jnp.exp(v - vals[0]) for v in vals]
    tot = es[0] + es[1] + es[2] + es[3]
    p_out = jnp.zeros((tm, LANES), F32)
    for k in range(TOP_K):
        p_out = jnp.where(wide == k, es[k] / tot, p_out)
    idx_ref[...] = idx_out
    p_ref[...] = p_out

    @pl.when(i == 0)
    def _():
        base_scr[...] = jnp.zeros(base_scr.shape, F32)

    base = base_scr[0:1, :]
    rb = min(RANK_BLOCK, tm)
    ns = TOP_K * rb
    before = (lax.broadcasted_iota(jnp.int32, (ns, ns), 1)
              < lax.broadcasted_iota(jnp.int32, (ns, ns), 0)).astype(BF16)
    wide_rb = lax.broadcasted_iota(jnp.int32, (rb, LANES), 1)
    ranks = []
    for s in range(tm // rb):
        onehot = jnp.concatenate([(wide_rb == sel[s * rb:(s + 1) * rb]).astype(F32) for sel in sels], axis=0)
        earlier = _dot(before, onehot.astype(BF16)) + base
        rank = jnp.sum(onehot * earlier, axis=-1, keepdims=True).astype(jnp.int32)
        base = base + jnp.sum(onehot, axis=0, keepdims=True)
        r_out = jnp.zeros((rb, LANES), jnp.int32)
        for k in range(TOP_K):
            r_out = jnp.where(wide_rb == k, rank[k * rb:(k + 1) * rb], r_out)
        ranks.append(r_out)
    rank_ref[...] = jnp.concatenate(ranks, axis=0)
    base_scr[...] = jnp.broadcast_to(base, base_scr.shape)
    cnt_ref[...] = jnp.broadcast_to(base, cnt_ref.shape)


def _norm_router(x, g, mod3, wr, br, geom, l):
    n, d = x.shape
    tm = geom.tm
    n_exp = wr.shape[1]
    return pl.pallas_call(
        functools.partial(_norm_router_body, geom=geom, n_exp=n_exp),
        grid=(n // tm,),
        in_specs=[pl.BlockSpec((tm, d), lambda i: (i, 0)),
                  pl.BlockSpec((1, d), lambda i: (l, 0)),
                  pl.BlockSpec(mod3.shape, lambda i: (0, 0, 0)),
                  pl.BlockSpec((d, n_exp), lambda i: (l, 0)),
                  pl.BlockSpec((1, n_exp), lambda i: (l, 0))],
        out_specs=[pl.BlockSpec((tm, d), lambda i: (i, 0)),
                   pl.BlockSpec((tm, LANES), lambda i: (i, 0)),
                   pl.BlockSpec((tm, LANES), lambda i: (i, 0)),
                   pl.BlockSpec((tm, LANES), lambda i: (i, 0)),
                   pl.BlockSpec((8, LANES), lambda i: (0, 0))],
        out_shape=[jax.ShapeDtypeStruct((n, d), F32),
                   jax.ShapeDtypeStruct((n, LANES), jnp.int32),
                   jax.ShapeDtypeStruct((n, LANES), F32),
                   jax.ShapeDtypeStruct((n, LANES), jnp.int32),
                   jax.ShapeDtypeStruct((8, LANES), F32)],
        scratch_shapes=[pltpu.VMEM((8, LANES), F32)],
        compiler_params=_cp(("arbitrary",)),
        name="norm_router",
    )(x, g, mod3, wr, br)


def _row_copy(src, dst, sem, s_row, d_slot):
    return pltpu.make_async_copy(src.at[pl.ds(s_row, 1), :], dst.at[pl.ds(d_slot, 1), :], sem)


DMA_UNROLL = 8


def _row_gather_step(idx_ref, src_ref, buf, sem, n_rows, n_steps):
    i = pl.program_id(0)

    def issue(step, slot):
        def start(r2, carry):
            for pr in range(2):
                r = 2 * r2 + pr
                _row_copy(src_ref, buf.at[slot], sem.at[slot], idx_ref[step * n_rows + r], r).start(priority=pr)
            return carry

        lax.fori_loop(0, n_rows // 2, start, 0, unroll=DMA_UNROLL // 2)

    @pl.when(i == 0)
    def _():
        issue(0, 0)

    @pl.when(i + 1 < n_steps)
    def _():
        issue(i + 1, (i + 1) % 2)

    slot = i % 2

    def wait(r, carry):
        _row_copy(src_ref, buf.at[slot], sem.at[slot], 0, r).wait()
        return carry

    lax.fori_loop(0, n_rows, wait, 0, unroll=DMA_UNROLL)
    return slot


def _gather_body(rows_ref, src_ref, o_ref, buf, sem, *, n_rows, n_steps):
    slot = _row_gather_step(rows_ref, src_ref, buf, sem, n_rows, n_steps)
    o_ref[...] = buf[slot].astype(o_ref.dtype)


def _moe_gather(rows, h2, tm):
    p_max = rows.shape[0]
    d = h2.shape[1]
    return pl.pallas_call(
        functools.partial(_gather_body, n_rows=tm, n_steps=p_max // tm),
        grid_spec=pltpu.PrefetchScalarGridSpec(
            num_scalar_prefetch=1,
            grid=(p_max // tm,),
            in_specs=[pl.BlockSpec(memory_space=pl.ANY)],
            out_specs=pl.BlockSpec((tm, d), lambda i, rows: (i, 0)),
            scratch_shapes=[pltpu.VMEM((2, tm, d), F32), pltpu.SemaphoreType.DMA((2,))]),
        out_shape=jax.ShapeDtypeStruct((p_max, d), BF16),
        compiler_params=_cp(("arbitrary",)),
        name="moe_gather",
    )(rows, h2)


def _expert_weights(plan, w_hbm, wbuf, wbf, sem, col_blocks, tn, e0, n_pass):
    blk_e, _, first, grp, e_next, meta, _ = plan
    j = pl.program_id(0)
    i = pl.program_id(1)
    n_grp = meta[1]

    def copies(e_idx, jj):
        return [pltpu.make_async_copy(w_hbm.at[e0 + e_idx, :, pl.ds(cb * tn, tn)], wbuf.at[p], sem.at[p])
                for p, cb in enumerate(col_blocks(jj))]

    @pl.when(first[i] == 1)
    def _():
        g = grp[i]

        @pl.when((j == 0) & (g == 0))
        def _():
            for c in copies(blk_e[0], 0):
                c.start()

        for c in copies(blk_e[i], j):
            c.wait()
        rows_per = 512 if wbf.shape[1] % 512 == 0 else wbf.shape[1]

        def cast_rows(r, carry):
            rows = pl.ds(pl.multiple_of(r * rows_per, rows_per), rows_per)
            for p in range(wbf.shape[0]):
                wbf[p, rows, :] = wbuf[p, rows, :].astype(BF16)
            return carry

        lax.fori_loop(0, wbf.shape[1] // rows_per, cast_rows, 0)
        last_in_pass = g == n_grp - 1

        @pl.when(jnp.logical_not(last_in_pass & (j == n_pass - 1)))
        def _():
            for c in copies(e_next[i], jnp.where(last_in_pass, j + 1, j)):
                c.start()


N_PLAN = 7


def _for_tile_rows(plan, o_ref, compute):
    i = pl.program_id(1)
    nv = plan[5][0]
    half = plan[6][i] == 1
    tm = o_ref.shape[0]

    @pl.when((i < nv) & jnp.logical_not(half))
    def _():
        compute(pl.ds(0, tm))

    @pl.when((i < nv) & half)
    def _():
        compute(pl.ds(0, tm // 2))
        o_ref[pl.ds(tm // 2, tm // 2), :] = jnp.zeros((tm // 2, o_ref.shape[1]), o_ref.dtype)

    @pl.when(i >= nv)
    def _():
        o_ref[...] = jnp.zeros(o_ref.shape, o_ref.dtype)


def _moe_gu_body(*refs, tn, e0, nj):
    plan, (x_ref, w_hbm, bg_ref, bu_ref, o_ref, wbuf, wbf, sem) = refs[:N_PLAN], refs[N_PLAN:]
    _expert_weights(plan, w_hbm, wbuf, wbf, sem, lambda jj: (jj, nj + jj), tn, e0, nj)

    def compute(rows):
        x = x_ref[rows, :]
        g = jnp.minimum(_dot(x, wbf[0]) + bg_ref[0], SWIGLU_LIMIT)
        u = jnp.clip(_dot(x, wbf[1]) + bu_ref[0], -SWIGLU_LIMIT, SWIGLU_LIMIT)
        o_ref[rows, :] = ((u + 1.0) * g * _sigmoid(SWIGLU_ALPHA * g)).astype(o_ref.dtype)

    _for_tile_rows(plan, o_ref, compute)


def _moe_gate_up(plan, xs, w, b, l, n_exp, tm):
    p_max, d = xs.shape
    de = w.shape[2] // 2
    tn = _pick(de, 512)
    nj = de // tn
    e0 = l * n_exp
    return pl.pallas_call(
        functools.partial(_moe_gu_body, tn=tn, e0=e0, nj=nj),
        grid_spec=pltpu.PrefetchScalarGridSpec(
            num_scalar_prefetch=N_PLAN,
            grid=(nj, p_max // tm),
            in_specs=[pl.BlockSpec((tm, d), lambda j, i, be, rb, *_: (rb[i], 0)),
                      pl.BlockSpec(memory_space=pl.ANY),
                      pl.BlockSpec((1, 1, tn), lambda j, i, be, *_: (e0 + be[i], 0, j)),
                      pl.BlockSpec((1, 1, tn), lambda j, i, be, *_: (e0 + be[i], 0, nj + j))],
            out_specs=pl.BlockSpec((tm, tn), lambda j, i, *_: (i, j)),
            scratch_shapes=[pltpu.VMEM((2, d, tn), F32), pltpu.VMEM((2, d, tn), BF16),
                            pltpu.SemaphoreType.DMA((2,))]),
        out_shape=jax.ShapeDtypeStruct((p_max, de), BF16),
        compiler_params=_cp(("arbitrary", "arbitrary")),
        name="moe_gate_up",
    )(*plan, xs, w, b, b)


def _moe_down_body(*refs, tn, e0, nj):
    plan, (x_ref, w_hbm, b_ref, o_ref, wbuf, wbf, sem) = refs[:N_PLAN], refs[N_PLAN:]
    _expert_weights(plan, w_hbm, wbuf, wbf, sem, lambda jj: (jj,), tn, e0, nj)

    def compute(rows):
        o_ref[rows, :] = _dot(x_ref[rows, :], wbf[0]) + b_ref[0]

    _for_tile_rows(plan, o_ref, compute)


def _moe_down(plan, act, w, b, l, n_exp, tm):
    p_max, de = act.shape
    d = w.shape[2]
    tn = _pick(d, 1024)
    nj = d // tn
    e0 = l * n_exp
    return pl.pallas_call(
        functools.partial(_moe_down_body, tn=tn, e0=e0, nj=nj),
        grid_spec=pltpu.PrefetchScalarGridSpec(
            num_scalar_prefetch=N_PLAN,
            grid=(nj, p_max // tm),
            in_specs=[pl.BlockSpec((tm, de), lambda j, i, be, rb, *_: (rb[i], 0)),
                      pl.BlockSpec(memory_space=pl.ANY),
                      pl.BlockSpec((1, 1, tn), lambda j, i, be, *_: (e0 + be[i], 0, j))],
            out_specs=pl.BlockSpec((tm, tn), lambda j, i, *_: (i, j)),
            scratch_shapes=[pltpu.VMEM((1, de, tn), F32), pltpu.VMEM((1, de, tn), BF16),
                            pltpu.SemaphoreType.DMA((1,))]),
        out_shape=jax.ShapeDtypeStruct((p_max, d), F32),
        compiler_params=_cp(("arbitrary", "arbitrary")),
        name="moe_down",
    )(*plan, act, w, b)


def _combine_body(dest_ref, yb_ref, x_ref, p_ref, mod_ref, op_ref, os_ref, buf, sem, *, geom, n_steps):
    i = pl.program_id(0)
    slot = _row_gather_step(dest_ref, yb_ref, buf, sem, ROW_BLOCK * TOP_K, n_steps)
    b = geom.bidx(i)
    p = p_ref[...]
    acc = p[:, 0:1] * buf[slot, pl.ds(0, ROW_BLOCK), :]
    for k in range(1, TOP_K):
        acc = acc + p[:, k:k + 1] * buf[slot, pl.ds(k * ROW_BLOCK, ROW_BLOCK), :]
    val = x_ref[...] + mod_ref[b, pl.ds(5, 1), :] * acc
    n_pb = geom.n_p // ROW_BLOCK

    @pl.when(i < n_pb)
    def _():
        op_ref[...] = val

    @pl.when(i >= n_pb)
    def _():
        os_ref[...] = val


def _moe_combine(dest_km, yb, x1, probs, mod3, geom):
    n, d = x1.shape
    n_pb = geom.n_p // ROW_BLOCK
    return pl.pallas_call(
        functools.partial(_combine_body, geom=geom, n_steps=n // ROW_BLOCK),
        grid_spec=pltpu.PrefetchScalarGridSpec(
            num_scalar_prefetch=1,
            grid=(n // ROW_BLOCK,),
            in_specs=[pl.BlockSpec(memory_space=pl.ANY),
                      pl.BlockSpec((ROW_BLOCK, d), lambda i, dest: (i, 0)),
                      pl.BlockSpec((ROW_BLOCK, LANES), lambda i, dest: (i, 0)),
                      pl.BlockSpec(mod3.shape, lambda i, dest: (0, 0, 0))],
            out_specs=[pl.BlockSpec((ROW_BLOCK, d), lambda i, dest: (jnp.minimum(i, n_pb - 1), 0)),
                       pl.BlockSpec((ROW_BLOCK, d), lambda i, dest: (jnp.maximum(i - n_pb, 0), 0))],
            scratch_shapes=[pltpu.VMEM((2, ROW_BLOCK * TOP_K, d), F32), pltpu.SemaphoreType.DMA((2,))]),
        out_shape=[jax.ShapeDtypeStruct((geom.n_p, d), F32), jax.ShapeDtypeStruct((geom.n_s, d), F32)],
        compiler_params=_cp(("arbitrary",)),
        name="moe_combine",
    )(dest_km, yb, x1, probs, mod3)


def _moe_plan(top_idx, rank, counts, n_exp, tm):
    n_tok = top_idx.shape[0]
    n_slot = n_tok * TOP_K
    flat_e = top_idx.reshape(n_slot)
    counts = counts.astype(jnp.int32)
    p_counts = (counts + tm - 1) // tm * tm
    p_ends = jnp.cumsum(p_counts)
    p_starts = p_ends - p_counts
    onehot = flat_e[:, None] == jnp.arange(n_exp, dtype=jnp.int32)[None, :]
    dest = (jnp.sum(jnp.where(onehot, p_starts[None, :], 0), axis=1) + rank.reshape(n_slot)).astype(jnp.int32)
    n_tiles = -(-n_slot // tm) + n_exp
    p_max = n_tiles * tm
    flat_tok = jnp.arange(n_slot, dtype=jnp.int32) // TOP_K
    rows = jnp.zeros((p_max,), jnp.int32).at[dest].set(flat_tok, unique_indices=True)
    tile_start = jnp.arange(n_tiles, dtype=jnp.int32) * tm
    nv = (p_ends[-1] // tm).astype(jnp.int32)
    blk_e = jnp.minimum(jnp.searchsorted(p_ends, tile_start, side="right"), n_exp - 1).astype(jnp.int32)
    last_e = blk_e[jnp.maximum(nv - 1, 0)]
    tile_id = jnp.arange(n_tiles, dtype=jnp.int32)
    blk_e = jnp.where(tile_id < nv, blk_e, last_e)
    row_blk = jnp.minimum(tile_id, jnp.maximum(nv - 1, 0))
    first = (tile_id < nv) & ((tile_id == 0) | (blk_e != jnp.roll(blk_e, 1)))
    grp = jnp.cumsum(first.astype(jnp.int32)) - 1
    n_grp = jnp.sum(first.astype(jnp.int32))
    starts = jnp.sort(jnp.where(first, tile_id, n_tiles))
    grp_e = blk_e[jnp.minimum(starts, n_tiles - 1)]
    e_next = grp_e[(grp + 1) % n_grp]
    meta = jnp.stack([nv, n_grp]).astype(jnp.int32)
    live = jnp.minimum(counts[blk_e] - (tile_start - p_starts[blk_e]), tm)
    half = ((tile_id < nv) & (live <= tm // 2)).astype(jnp.int32)
    plan = (blk_e, row_blk, first.astype(jnp.int32), grp.astype(jnp.int32), e_next.astype(jnp.int32), meta, half)
    dest_km = dest.reshape(n_tok // ROW_BLOCK, ROW_BLOCK, TOP_K).transpose(0, 2, 1).reshape(n_slot)
    return rows, dest_km, plan


def kernel(x_prompt, x_sample, cache_k, cache_v, state_ssm_re, state_ssm_im, c_prompt, c_sample,
           w_ada, b_ada, g_norm_mix, g_norm_ffn, w_in, b_in, g_q, g_k, w_attn_up,
           ssm_a_re, ssm_a_im, ssm_log_dt, ssm_b_re, ssm_b_im, ssm_c_re, ssm_c_im, ssm_d,
           w_glu, w_out, w_router, b_router, w_gate_up, b_gate_up, w_down, b_down):
    B, T, D = x_prompt.shape
    DB, DT, _ = x_sample.shape
    depth = w_ada.shape[0]
    H, hd = cache_k.shape[3], cache_k.shape[4]
    AW = H * hd
    G, N, P = ssm_b_re.shape[1:]
    SW = G * P
    E = w_router.shape[2]
    assert hd == LANES and P == 16 and 2 * N == LANES and DT == ROW_BLOCK
    geom = _Geom(B, T, DB, DT)
    n_p, n_tok, nbp = geom.n_p, geom.n_tok, geom.nbp
    L = SSM_CHUNK
    assert T % L == 0 and DT % L == 0
    moe_tm = 512
    gather_rows = 256

    xp, xs_ = x_prompt.reshape(n_p, D), x_sample.reshape(geom.n_s, D)
    c_all = jnp.concatenate([c_prompt, c_sample, jnp.zeros((nbp - geom.nb, D), F32)], axis=0)

    flat2 = lambda w: w.reshape((w.shape[0] * w.shape[1],) + w.shape[2:])
    w_ada2, w_in2, w_au2, w_glu2, w_out2, w_r2 = map(flat2, (w_ada, w_in, w_attn_up, w_glu, w_out, w_router))
    w_gu3, w_dn3 = flat2(w_gate_up), flat2(w_down)
    b_gu3 = b_gate_up.reshape(depth * E, 1, b_gate_up.shape[2])
    b_dn3 = b_down.reshape(depth * E, 1, b_down.shape[2])

    NG = LANES // P
    assert G % NG == 0
    GB, NS = G // NG, NG * N
    lam = jnp.stack([ssm_a_re.reshape(depth * GB, NS), ssm_a_im.reshape(depth * GB, NS)], axis=1)
    ldt = jnp.repeat(ssm_log_dt, N, axis=-1).reshape(depth * GB, 1, NS)
    bb = jnp.stack([ssm_b_re.reshape(depth * GB, NS, P), ssm_b_im.reshape(depth * GB, NS, P)], axis=1)
    cb = jnp.stack([ssm_c_re.reshape(depth * GB, LANES, N), ssm_c_im.reshape(depth * GB, LANES, N)], axis=1)
    db = ssm_d.reshape(depth * GB, 1, LANES)

    nc_p, nc_s = n_p // L, geom.n_s // L
    NC = nc_p + nc_s
    cps, css = T // L, DT // L
    lidx = jnp.concatenate([jnp.arange(nc_p, dtype=jnp.int32) % cps,
                            jnp.arange(nc_s, dtype=jnp.int32) % css]).reshape(NC, 1)
    n_steps = max(cps - 1, css - 1, 1).bit_length()
    last_rows = tuple((b + 1) * cps - 1 for b in range(B)) + tuple(nc_p + (b + 1) * css - 1 for b in range(DB))
    first_rows = tuple(nc_p + b * css for b in range(DB))

    ks, vs, srs, sis = [], [], [], []
    for l in range(depth):
        mod = _ada(c_all, w_ada2, b_ada, l)
        mod3 = mod.reshape(nbp, 6, D)
        h = _norm_mod(xp, xs_, g_norm_mix, mod3, geom, l, 0, 1)

        dn = functools.partial(_dense, h, w_in2, l, geom=geom, bias=b_in)
        split_f32 = ((F32, "p"), (F32, "s"), (BF16, "all"))
        (qb,) = dn(0, AW, mode="q", extra=(g_q,), extra_kinds=("row",), out_kinds=((BF16, "all"),), name="proj_q")
        kfp, kfs, kb = dn(AW, AW, mode="k", extra=(g_k,), extra_kinds=("row",), out_kinds=split_f32, name="proj_k")
        vfp, vfs, vb = dn(2 * AW, AW, mode="v", out_kinds=split_f32, name="proj_v")
        (u,) = dn(3 * AW, SW, mode="plain", name="proj_u")
        (sga,) = dn(3 * AW + SW, D, mode="sig", name="proj_ga")
        (sgb,) = dn(3 * AW + SW + D, D, mode="sig", name="proj_gb")

        attn = jnp.concatenate([_sb_prompt(qb, kb, vb, B, T, H, hd),
                                _sb_sample(qb, kb, vb, cache_k, cache_v, l, DB, DT, H, hd, n_p)], axis=0)
        (ma,) = _dense(attn, w_au2, l, 0, D, mode="mul", geom=geom, extra=(sga,), extra_kinds=("tile",),
                       name="attn_up")

        x0 = jnp.concatenate([state_ssm_re[l].reshape(DB, GB, NS), state_ssm_im[l].reshape(DB, GB, NS)],
                             axis=-1).transpose(1, 0, 2)
        yg, fin = _s5(u, x0, lidx, lam, ldt, bb, cb, db, l, N, n_steps, first_rows, last_rows)
        fin = fin.reshape(GB, geom.nb, 2, NG, N).transpose(2, 1, 0, 3, 4).reshape(2, geom.nb, G, N)

        merged = _glu_merge(yg, w_glu2, l, ma, sgb, geom)
        (x1,) = _dense(merged, w_out2, l, 0, D, mode="resid", geom=geom, extra=(xp, xs_, mod3),
                       extra_kinds=("tile_p", "tile_s", "mod"), name="out_proj")

        h2, top_idx, probs, rank, cnt = _norm_router(x1, g_norm_ffn, mod3, w_r2, b_router, geom, l)
        rows, dest_km, plan = _moe_plan(top_idx[:, :TOP_K], rank[:, :TOP_K], cnt[0, :E], E, moe_tm)
        xs = _moe_gather(rows, h2, gather_rows)
        act = _moe_gate_up(plan, xs, w_gu3, b_gu3, l, E, moe_tm)
        yb = _moe_down(plan, act, w_dn3, b_dn3, l, E, moe_tm)
        xp, xs_ = _moe_combine(dest_km, yb, x1, probs, mod3, geom)

        ks.append((kfp, kfs))
        vs.append((vfp, vfs))
        srs.append(fin[0])
        sis.append(fin[1])

    def stacked(pairs, part, lead):
        return jnp.stack([p[part] for p in pairs]).reshape((depth,) + lead + (H, hd))

    sr, si = jnp.stack(srs), jnp.stack(sis)
    return (xp.reshape(B, T, D), xs_.reshape(DB, DT, D), stacked(ks, 0, (B, T)), stacked(vs, 0, (B, T)),
            sr[:, :B], si[:, :B], stacked(ks, 1, (DB, DT)), stacked(vs, 1, (DB, DT)), sr[:, B:], si[:, B:])
```

```python
import functools
import math

import jax
import jax.numpy as jnp
from jax import lax
from jax.experimental import pallas as pl
from jax.experimental.pallas import tpu as pltpu

F32 = jnp.float32
BF16 = jnp.bfloat16

NORM_EPS = 1e-6
TOP_K = 4
SWIGLU_LIMIT = 7.0
SWIGLU_ALPHA = 1.702
ROW_BLOCK = 64
SSM_CHUNK = 16
LANES = 128
VMEM_LIMIT = 60 * 1024 * 1024


def _cp(sem, vmem=VMEM_LIMIT):
    return pltpu.CompilerParams(dimension_semantics=sem, vmem_limit_bytes=vmem)


def _pick(n, pref):
    t = pref
    while t > 1 and n % t:
        t //= 2
    return t


def _sigmoid(x):
    return 1.0 / (1.0 + jnp.exp(-x))


def _dot(a, b):
    return jnp.dot(a, b, preferred_element_type=F32)


def _split(a):
    hi = a.astype(BF16)
    lo = (a - hi.astype(F32)).astype(BF16)
    return hi, lo


def _dot3(a, b):
    ah, al = _split(a)
    bh, bl = _split(b)
    return _dot(ah, bh) + (_dot(al, bh) + _dot(ah, bl))


class _Geom:
    def __init__(self, B, T, DB, DT):
        self.B, self.T, self.DB, self.DT = B, T, DB, DT
        self.n_p, self.n_s = B * T, DB * DT
        self.n_tok = self.n_p + self.n_s
        assert T % ROW_BLOCK == 0 and DT % ROW_BLOCK == 0
        self.tm = 512
        while self.n_p % self.tm or self.n_s % self.tm:
            self.tm //= 2
        assert self.tm >= ROW_BLOCK
        self.sub = self.tm // ROW_BLOCK
        self.npt = self.n_p // self.tm
        self.nb = B + DB
        self.nbp = -(-self.nb // 8) * 8

    def p_blk(self, i):
        return jnp.minimum(i, self.npt - 1)

    def s_blk(self, i):
        return jnp.maximum(i - self.npt, 0)

    def bidx(self, blk):
        pb = self.n_p // ROW_BLOCK
        return jnp.where(blk < pb, blk // (self.T // ROW_BLOCK),
                         self.B + (blk - pb) // (self.DT // ROW_BLOCK))


def _ada_body(c_ref, w_ref, b_ref, o_ref):
    c = c_ref[...]
    s = c * _sigmoid(c)
    o_ref[...] = _dot(s.astype(BF16), w_ref[...].astype(BF16)) + b_ref[...]


def _ada(c_pad, w, b, l):
    nbp, d = c_pad.shape
    n = w.shape[1]
    tn = _pick(n, 1024)
    return pl.pallas_call(
        _ada_body,
        grid=(n // tn,),
        in_specs=[pl.BlockSpec((nbp, d), lambda j: (0, 0)),
                  pl.BlockSpec((d, tn), lambda j: (l, j)),
                  pl.BlockSpec((1, tn), lambda j: (l, j))],
        out_specs=pl.BlockSpec((nbp, tn), lambda j: (0, j)),
        out_shape=jax.ShapeDtypeStruct((nbp, n), F32),
        compiler_params=_cp(("arbitrary",)),
        name="ada_mod",
    )(c_pad, w, b)


def _norm_rows(x, g, sc, sh):
    ms = jnp.mean(x * x, axis=-1, keepdims=True)
    return (x * lax.rsqrt(ms + NORM_EPS)) * g * (1.0 + sc) + sh


def _load_rows(i, geom, ref_p, ref_s, rows):
    return jnp.where(i < geom.npt, ref_p[rows, :], ref_s[rows, :])


def _store_tile(i, geom, ref_p, ref_s, val):
    @pl.when(i < geom.npt)
    def _():
        ref_p[...] = val

    @pl.when(i >= geom.npt)
    def _():
        ref_s[...] = val


def _norm_mod_body(xp_ref, xs_ref, g_ref, mod_ref, o_ref, *, geom, i_sh, i_sc):
    i = pl.program_id(0)
    for r in range(geom.sub):
        b = geom.bidx(i * geom.sub + r)
        rows = pl.ds(r * ROW_BLOCK, ROW_BLOCK)
        h = _norm_rows(_load_rows(i, geom, xp_ref, xs_ref, rows), g_ref[...], mod_ref[b, pl.ds(i_sc, 1), :],
                       mod_ref[b, pl.ds(i_sh, 1), :])
        o_ref[rows, :] = h.astype(o_ref.dtype)


def _norm_mod(xp, xs, g, mod3, geom, l, i_sh, i_sc):
    d = xp.shape[1]
    tm = geom.tm
    return pl.pallas_call(
        functools.partial(_norm_mod_body, geom=geom, i_sh=i_sh, i_sc=i_sc),
        grid=(geom.n_tok // tm,),
        in_specs=[pl.BlockSpec((tm, d), lambda i: (geom.p_blk(i), 0)),
                  pl.BlockSpec((tm, d), lambda i: (geom.s_blk(i), 0)),
                  pl.BlockSpec((1, d), lambda i: (l, 0)),
                  pl.BlockSpec(mod3.shape, lambda i: (0, 0, 0))],
        out_specs=pl.BlockSpec((tm, d), lambda i: (i, 0)),
        out_shape=jax.ShapeDtypeStruct((geom.n_tok, d), BF16),
        compiler_params=_cp(("arbitrary",)),
        name="norm_mod",
    )(xp, xs, g, mod3)


def _head_norm(acc, g):
    outs = []
    for c in range(acc.shape[1] // LANES):
        a = acc[:, c * LANES:(c + 1) * LANES]
        ms = jnp.mean(a * a, axis=-1, keepdims=True)
        outs.append(a * lax.rsqrt(ms + NORM_EPS) * g)
    return jnp.concatenate(outs, axis=1)


def _dense_body(*refs, mode, geom, n_extra, n_out, has_bias):
    x_ref, w_ref = refs[0], refs[1]
    pos = 2
    b_ref = None
    if has_bias:
        b_ref = refs[pos]
        pos += 1
    extra = refs[pos:pos + n_extra]
    outs = refs[pos + n_extra:pos + n_extra + n_out]
    wb = refs[pos + n_extra + n_out]
    i = pl.program_id(1)

    @pl.when(i == 0)
    def _():
        wb[...] = w_ref[...].astype(BF16)

    acc = _dot(x_ref[...], wb[...])
    if has_bias:
        acc = acc + b_ref[...]
    if mode == "q":
        qscale = LANES ** -0.5 * LOG2_E
        outs[0][...] = (_head_norm(acc, extra[0][...]) * qscale).astype(BF16)
    elif mode == "k":
        a = _head_norm(acc, extra[0][...])
        _store_tile(i, geom, outs[0], outs[1], a)
        outs[2][...] = a.astype(BF16)
    elif mode == "v":
        _store_tile(i, geom, outs[0], outs[1], acc)
        outs[2][...] = acc.astype(BF16)
    elif mode == "plain":
        outs[0][...] = acc
    elif mode == "sig":
        outs[0][...] = _sigmoid(acc)
    elif mode == "mul":
        outs[0][...] = acc * extra[0][...]
    elif mode == "resid":
        xres_p, xres_s, mod_ref = extra
        for r in range(geom.sub):
            b = geom.bidx(i * geom.sub + r)
            rows = pl.ds(r * ROW_BLOCK, ROW_BLOCK)
            gate = mod_ref[b, pl.ds(2, 1), :]
            outs[0][rows, :] = (_load_rows(i, geom, xres_p, xres_s, rows)
                                + gate * acc[r * ROW_BLOCK:(r + 1) * ROW_BLOCK, :])
    else:
        raise ValueError(mode)


def _dense(x, w, l, col0, ncols, *, mode, geom, bias=None, extra=(), extra_kinds=(), out_kinds=((F32, "all"),),
           tn_pref=1024, name="dense"):
    m, k = x.shape
    tm = geom.tm
    tn = _pick(math.gcd(ncols, col0) if col0 else ncols, tn_pref)
    cb = col0 // tn
    row_maps = {"all": lambda j, i: (i, j), "p": lambda j, i: (geom.p_blk(i), j),
                "s": lambda j, i: (geom.s_blk(i), j)}
    n_rows = {"all": m, "p": geom.n_p, "s": geom.n_s}
    in_specs = [pl.BlockSpec((tm, k), lambda j, i: (i, 0)),
                pl.BlockSpec((k, tn), lambda j, i: (l, cb + j), pipeline_mode=pl.Buffered(1))]
    args = [x, w]
    if bias is not None:
        in_specs.append(pl.BlockSpec((1, tn), lambda j, i: (l, cb + j)))
        args.append(bias)
    for a, kind in zip(extra, extra_kinds):
        if kind == "tile":
            in_specs.append(pl.BlockSpec((tm, tn), row_maps["all"]))
        elif kind in ("tile_p", "tile_s"):
            in_specs.append(pl.BlockSpec((tm, tn), row_maps[kind[-1]]))
        elif kind == "row":
            in_specs.append(pl.BlockSpec((1, a.shape[1]), lambda j, i: (l, 0)))
        elif kind == "mod":
            in_specs.append(pl.BlockSpec((a.shape[0], a.shape[1], tn), lambda j, i: (0, 0, j)))
        args.append(a)
    out_specs = [pl.BlockSpec((tm, tn), row_maps[part]) for _, part in out_kinds]
    out_shape = [jax.ShapeDtypeStruct((n_rows[part], ncols), dt) for dt, part in out_kinds]
    res = pl.pallas_call(
        functools.partial(_dense_body, mode=mode, geom=geom, n_extra=len(extra), n_out=len(out_kinds),
                          has_bias=bias is not None),
        grid=(ncols // tn, m // tm),
        in_specs=in_specs,
        out_specs=out_specs,
        out_shape=out_shape,
        scratch_shapes=[pltpu.VMEM((k, tn), BF16)],
        compiler_params=_cp(("arbitrary", "arbitrary")),
        name=name,
    )(*args)
    return res


def _glu_body(x_ref, wv_ref, wg_ref, ma_ref, sgb_ref, o_ref, wvb, wgb):
    @pl.when(pl.program_id(1) == 0)
    def _():
        wvb[...] = wv_ref[...].astype(BF16)
        wgb[...] = wg_ref[...].astype(BF16)

    x = x_ref[...].astype(BF16)
    val = _dot(x, wvb[...])
    gate = _dot(x, wgb[...])
    o_ref[...] = (ma_ref[...] + sgb_ref[...] * (val * _sigmoid(gate))).astype(o_ref.dtype)


def _glu_merge(yg, w, l, ma, sgb, geom):
    m, k = yg.shape
    d = ma.shape[1]
    tm = geom.tm
    tn = _pick(d, 1024)
    nj = d // tn
    return pl.pallas_call(
        _glu_body,
        grid=(nj, m // tm),
        in_specs=[pl.BlockSpec((tm, k), lambda j, i: (i, 0)),
                  pl.BlockSpec((k, tn), lambda j, i: (l, j), pipeline_mode=pl.Buffered(1)),
                  pl.BlockSpec((k, tn), lambda j, i: (l, nj + j), pipeline_mode=pl.Buffered(1)),
                  pl.BlockSpec((tm, tn), lambda j, i: (i, j)),
                  pl.BlockSpec((tm, tn), lambda j, i: (i, j))],
        out_specs=pl.BlockSpec((tm, tn), lambda j, i: (i, j)),
        out_shape=jax.ShapeDtypeStruct((m, d), BF16),
        scratch_shapes=[pltpu.VMEM((k, tn), BF16), pltpu.VMEM((k, tn), BF16)],
        compiler_params=_cp(("arbitrary", "arbitrary")),
        name="glu_merge",
    )(yg, w, w, ma, sgb)


LOG2_E = 1.4426950408889634


def _neg_abs(z):
    return lax.bitcast_convert_type(lax.bitcast_convert_type(z, jnp.uint32) | jnp.uint32(0x80000000), F32)


def _sb_tile(q, kt, vt, tri, mask, c, o):
    z = lax.dot_general(q, kt, (((1,), (1,)), ((), ())), preferred_element_type=F32)
    sp = jnp.maximum(z, 0.0) + jnp.log2(1.0 + jnp.exp2(_neg_abs(z)))
    if mask is not None:
        sp = jnp.where(mask, sp, 0.0)
    incl = _dot(sp.astype(BF16), tri)
    w = jnp.exp2(z - incl - c)
    if mask is not None:
        w = jnp.where(mask, w, 0.0)
    o = o + _dot(w.astype(BF16), vt)
    c = c + incl[:, 0:1]
    return c, o


def _suffix_matrix(tk):
    j = lax.broadcasted_iota(jnp.int32, (tk, tk), 0)
    s = lax.broadcasted_iota(jnp.int32, (tk, tk), 1)
    return (j >= s).astype(BF16)


def _sb_prompt_body(q_ref, k_ref, v_ref, o_ref, *, tq, tk):
    qi = pl.program_id(2)
    q = q_ref[...]
    r = tq // tk
    tri = _suffix_matrix(tk)
    row = lax.broadcasted_iota(jnp.int32, (tq, tk), 0)
    col = lax.broadcasted_iota(jnp.int32, (tq, tk), 1)
    c = jnp.zeros((tq, 1), F32)
    o = jnp.zeros((tq, q.shape[1]), F32)
    for d in reversed(range(r)):
        ks = pl.ds(pl.multiple_of((qi * r + d) * tk, tk), tk)
        c, o = _sb_tile(q, k_ref[ks, :], v_ref[ks, :], tri, col + d * tk < row, c, o)

    def body(it, carry):
        c, o = carry
        for u in range(r):
            ks = pl.ds(pl.multiple_of((qi * r - 1 - (it * r + u)) * tk, tk), tk)
            c, o = _sb_tile(q, k_ref[ks, :], v_ref[ks, :], tri, None, c, o)
        return c, o

    c, o = lax.fori_loop(0, qi, body, (c, o))
    o_ref[...] = o.astype(o_ref.dtype)


def _sb_prompt(q, kb, vb, B, T, H, hd):
    tq = _pick(T, 512)
    tk = _pick(tq, 256)
    nq = T // tq
    return pl.pallas_call(
        functools.partial(_sb_prompt_body, tq=tq, tk=tk),
        grid=(B, H, nq),
        in_specs=[pl.BlockSpec((tq, hd), lambda b, h, i: (b * nq + i, h)),
                  pl.BlockSpec((T, hd), lambda b, h, i: (b, h)),
                  pl.BlockSpec((T, hd), lambda b, h, i: (b, h))],
        out_specs=pl.BlockSpec((tq, hd), lambda b, h, i: (b * nq + i, h)),
        out_shape=jax.ShapeDtypeStruct((B * T, H * hd), BF16),
        compiler_params=_cp(("arbitrary", "arbitrary", "arbitrary")),
        name="sb_prompt",
    )(q, kb, vb)


def _sb_sample_body(q_ref, k_ref, v_ref, ck_ref, cv_ref, o_ref, *, dt, tk, past, hb, hd):
    row = lax.broadcasted_iota(jnp.int32, (dt, dt), 0)
    col = lax.broadcasted_iota(jnp.int32, (dt, dt), 1)
    tri_new = _suffix_matrix(dt)
    tri = _suffix_matrix(tk)
    n_t = past // tk
    heads = [slice(h * hd, (h + 1) * hd) for h in range(hb)]
    c0 = jnp.zeros((dt, 1), F32)
    o0 = jnp.zeros((dt, hd), F32)
    carry = tuple(_sb_tile(q_ref[:, sl], k_ref[:, sl], v_ref[:, sl], tri_new, col < row, c0, o0)
                  for sl in heads)

    def body(it, carry):
        ks = pl.ds(pl.multiple_of((n_t - 1 - it) * tk, tk), tk)
        return tuple(_sb_tile(q_ref[:, sl], ck_ref[0, 0, ks, h, :].astype(BF16), cv_ref[0, 0, ks, h, :].astype(BF16),
                              tri, None, co[0], co[1])
                     for h, (sl, co) in enumerate(zip(heads, carry)))

    carry = lax.fori_loop(0, n_t, body, carry)
    for sl, co in zip(heads, carry):
        o_ref[:, sl] = co[1].astype(o_ref.dtype)


def _sb_sample(q, kb, vb, ck, cv, l, DB, DT, H, hd, n_p):
    past = ck.shape[2]
    tk = _pick(past, 256)
    hb = _pick(H, 8)
    rb0 = n_p // DT
    new = lambda b, h: (rb0 + b, h)
    return pl.pallas_call(
        functools.partial(_sb_sample_body, dt=DT, tk=tk, past=past, hb=hb, hd=hd),
        grid=(DB, H // hb),
        in_specs=[pl.BlockSpec((DT, hb * hd), new),
                  pl.BlockSpec((DT, hb * hd), new),
                  pl.BlockSpec((DT, hb * hd), new),
                  pl.BlockSpec((1, 1, past, hb, hd), lambda b, h: (l, b, 0, h, 0)),
                  pl.BlockSpec((1, 1, past, hb, hd), lambda b, h: (l, b, 0, h, 0))],
        out_specs=pl.BlockSpec((DT, hb * hd), lambda b, h: (b, h)),
        out_shape=jax.ShapeDtypeStruct((DB * DT, H * hd), BF16),
        compiler_params=_cp(("arbitrary", "arbitrary")),
        name="sb_sample",
    )(q, kb, vb, ck, cv)


def _cexp(lr, li, t):
    mag = jnp.exp(lr * t)
    return mag * jnp.cos(li * t), mag * jnp.sin(li * t)


def _split3(a):
    hi = a.astype(BF16)
    r1 = a - hi.astype(F32)
    mid = r1.astype(BF16)
    return hi, mid, (r1 - mid.astype(F32)).astype(BF16)


NT_DIMS = (((1,), (1,)), ((), ()))


def _s5_body(u_ref, x0_ref, lidx_ref, lam_ref, ldt_ref, b_ref, c_ref, d_ref, y_ref, fin_ref,
             wst, vst, tbig, ucat, x0s, e_scr, *, n_state, n_steps, first_rows, last_rows):
    L, P, N = SSM_CHUNK, 16, n_state
    NS = (LANES // P) * N
    nc = ucat.shape[0]
    lr, li = lam_ref[0, 0:1, :], lam_ref[0, 1:2, :]
    dt = jnp.exp(ldt_ref[0])
    abr, abi = _cexp(lr, li, dt)
    den = lr * lr + li * li
    nr = abr - 1.0
    cor = (nr * lr + abi * li) / den
    coi = (abi * lr - nr * li) / den

    expand = (lax.broadcasted_iota(jnp.int32, (N, NS), 1) % N
              == lax.broadcasted_iota(jnp.int32, (N, NS), 0)).astype(BF16)
    own = (lax.broadcasted_iota(jnp.int32, (LANES, NS), 0) // P
           == lax.broadcasted_iota(jnp.int32, (LANES, NS), 1) // N)

    def block_diag(x):
        h, m, lo = _split3(x)
        return jnp.where(own, _dot(h, expand) + (_dot(m, expand) + _dot(lo, expand)), 0.0)

    spread = (lax.broadcasted_iota(jnp.int32, (LANES, P), 0) % P
              == lax.broadcasted_iota(jnp.int32, (LANES, P), 1)).astype(BF16)

    def block_diag_t(x):
        h, m, lo = _split3(x)
        nt = lambda a: lax.dot_general(spread, a, NT_DIMS, preferred_element_type=F32)
        return jnp.where(own, nt(h) + (nt(m) + nt(lo)), 0.0)

    br, bi = block_diag_t(b_ref[0, 0]), block_diag_t(b_ref[0, 1])
    cr, ci = block_diag(c_ref[0, 0]), block_diag(c_ref[0, 1])
    wr, wi = cor * br - coi * bi, cor * bi + coi * br
    vr, vi = cr * abr - ci * abi, cr * abi + ci * abr
    for m in range(L):
        wst[pl.ds((L - 1 - m) * LANES, LANES), :] = jnp.concatenate([wr, wi], axis=1).astype(BF16)
        vst[pl.ds(m * LANES, LANES), :] = jnp.concatenate([vr, -vi], axis=1).astype(BF16)
        if m + 1 < L:
            wr, wi = wr * abr - wi * abi, wr * abi + wi * abr
            vr, vi = vr * abr - vi * abi, vr * abi + vi * abr

    cc = jnp.concatenate([cr, -ci], axis=1).astype(BF16)
    t_all = lax.dot_general(wst[...], cc, NT_DIMS, preferred_element_type=F32)
    eye = (lax.broadcasted_iota(jnp.int32, (LANES, LANES), 0)
           == lax.broadcasted_iota(jnp.int32, (LANES, LANES), 1))
    t0 = t_all[(L - 1) * LANES:, :] + jnp.where(eye, d_ref[0], 0.0)
    t_bf = jnp.concatenate([t_all[:(L - 1) * LANES, :], t0], axis=0).astype(BF16)
    for i in range(L):
        cols = pl.ds(i * LANES, LANES)
        tbig[pl.ds(0, (i + 1) * LANES), cols] = t_bf[(L - 1 - i) * LANES:, :]
        if i + 1 < L:
            tbig[pl.ds((i + 1) * LANES, (L - 1 - i) * LANES), cols] = jnp.zeros(((L - 1 - i) * LANES, LANES), BF16)

    for j in range(L):
        ucat[:, pl.ds(j * LANES, LANES)] = u_ref[pl.ds(j, nc, stride=L), :].astype(BF16)
    uc = ucat[...]
    s_loc = _dot(uc, wst[...])

    x0s[...] = jnp.zeros(x0s.shape, F32)
    for b, row in enumerate(first_rows):
        x0s[pl.ds(row, 1), :] = x0_ref[0, pl.ds(b, 1), :]
    x0 = x0s[...]

    def cmul(t, s):
        pr, pi = _cexp(lr, li, t * dt)
        sr, si = s[:, :NS], s[:, NS:]
        return jnp.concatenate([pr * sr - pi * si, pr * si + pi * sr], axis=1)

    lidx = lidx_ref[...]
    e = s_loc + cmul(float(L), x0)
    for kk in range(n_steps):
        sh = pltpu.roll(e, 1 << kk, axis=0)
        e = e + jnp.where(lidx >= (1 << kk), cmul(float(L * (1 << kk)), sh), 0.0)
    s_in = jnp.where(lidx >= 1, pltpu.roll(e, 1, axis=0), x0)
    y = _dot(uc, tbig[...]) + lax.dot_general(s_in.astype(BF16), vst[...], NT_DIMS, preferred_element_type=F32)
    g = 0.5 * y * (1.0 + lax.erf(y * (2.0 ** -0.5)))
    for i in range(L):
        y_ref[pl.ds(i, nc, stride=L), :] = g[:, i * LANES:(i + 1) * LANES]
    e_scr[...] = e
    for r, src in enumerate(last_rows):
        fin_ref[0, pl.ds(r, 1), :] = e_scr[pl.ds(src, 1), :]


def _s5(u, x0, lidx, lam, ldt, bb, cb, db, l, n_state, n_steps, first_rows, last_rows):
    n_tok, sw = u.shape
    gb = sw // LANES
    nc = n_tok // SSM_CHUNK
    ns2 = 2 * (LANES // 16) * n_state
    nb = len(last_rows)
    w = SSM_CHUNK * LANES
    blk3 = lambda g: (l * gb + g, 0, 0)
    blk4 = lambda g: (l * gb + g, 0, 0, 0)
    return pl.pallas_call(
        functools.partial(_s5_body, n_state=n_state, n_steps=n_steps, first_rows=first_rows, last_rows=last_rows),
        grid=(gb,),
        in_specs=[pl.BlockSpec((n_tok, LANES), lambda g: (0, g)),
                  pl.BlockSpec((1, x0.shape[1], ns2), lambda g: (g, 0, 0)),
                  pl.BlockSpec((nc, 1), lambda g: (0, 0)),
                  pl.BlockSpec((1, 2, ns2 // 2), blk3),
                  pl.BlockSpec((1, 1, ns2 // 2), blk3),
                  pl.BlockSpec((1, 2, ns2 // 2, 16), blk4),
                  pl.BlockSpec((1, 2, LANES, n_state), blk4),
                  pl.BlockSpec((1, 1, LANES), blk3)],
        out_specs=[pl.BlockSpec((n_tok, LANES), lambda g: (0, g)),
                   pl.BlockSpec((1, nb, ns2), lambda g: (g, 0, 0))],
        out_shape=[jax.ShapeDtypeStruct((n_tok, sw), F32),
                   jax.ShapeDtypeStruct((gb, nb, ns2), F32)],
        scratch_shapes=[pltpu.VMEM((w, ns2), BF16), pltpu.VMEM((w, ns2), BF16), pltpu.VMEM((w, w), BF16),
                        pltpu.VMEM((nc, w), BF16), pltpu.VMEM((nc, ns2), F32), pltpu.VMEM((nc, ns2), F32)],
        compiler_params=_cp(("arbitrary",)),
        name="s5_block",
    )(u, x0, lidx, lam, ldt, bb, cb, db)


RANK_BLOCK = 128


def _norm_router_body(x_ref, g_ref, mod_ref, wr_ref, br_ref, h_ref, idx_ref, p_ref, rank_ref, cnt_ref, base_scr,
                      *, geom, n_exp):
    i = pl.program_id(0)
    for r in range(geom.sub):
        b = geom.bidx(i * geom.sub + r)
        rows = pl.ds(r * ROW_BLOCK, ROW_BLOCK)
        h_ref[rows, :] = _norm_rows(x_ref[rows, :], g_ref[...], mod_ref[b, pl.ds(4, 1), :],
                                    mod_ref[b, pl.ds(3, 1), :])
    logits = _dot3(h_ref[...], wr_ref[...]) + br_ref[...]
    tm = logits.shape[0]
    lane = lax.broadcasted_iota(jnp.int32, (tm, n_exp), 1)
    wide = lax.broadcasted_iota(jnp.int32, (tm, LANES), 1)
    idx_out = jnp.zeros((tm, LANES), jnp.int32)
    vals, sels = [], []
    for k in range(TOP_K):
        m = jnp.max(logits, axis=-1, keepdims=True)
        sel = jnp.min(jnp.where(logits == m, lane, n_exp), axis=-1, keepdims=True)
        idx_out = jnp.where(wide == k, sel, idx_out)
        vals.append(m)
        sels.append(sel)
        logits = jnp.where(lane == sel, -jnp.inf, logits)
    es = [jnp.exp(v - vals[0]) for v in vals]
    tot = es[0] + es[1] + es[2] + es[3]
    p_out = jnp.zeros((tm, LANES), F32)
    for k in range(TOP_K):
        p_out = jnp.where(wide == k, es[k] / tot, p_out)
    idx_ref[...] = idx_out
    p_ref[...] = p_out

    @pl.when(i == 0)
    def _():
        base_scr[...] = jnp.zeros(base_scr.shape, F32)

    base = base_scr[0:1, :]
    rb = min(RANK_BLOCK, tm)
    ns = TOP_K * rb
    before = (lax.broadcasted_iota(jnp.int32, (ns, ns), 1)
              < lax.broadcasted_iota(jnp.int32, (ns, ns), 0)).astype(BF16)
    wide_rb = lax.broadcasted_iota(jnp.int32, (rb, LANES), 1)
    ranks = []
    for s in range(tm // rb):
        onehot = jnp.concatenate([(wide_rb == sel[s * rb:(s + 1) * rb]).astype(F32) for sel in sels], axis=0)
        earlier = _dot(before, onehot.astype(BF16)) + base
        rank = jnp.sum(onehot * earlier, axis=-1, keepdims=True).astype(jnp.int32)
        base = base + jnp.sum(onehot, axis=0, keepdims=True)
        r_out = jnp.zeros((rb, LANES), jnp.int32)
        for k in range(TOP_K):
            r_out = jnp.where(wide_rb == k, rank[k * rb:(k + 1) * rb], r_out)
        ranks.append(r_out)
    rank_ref[...] = jnp.concatenate(ranks, axis=0)
    base_scr[...] = jnp.broadcast_to(base, base_scr.shape)
    cnt_ref[...] = jnp.broadcast_to(base, cnt_ref.shape)


def _norm_router(x, g, mod3, wr, br, geom, l):
    n, d = x.shape
    tm = geom.tm
    n_exp = wr.shape[1]
    return pl.pallas_call(
        functools.partial(_norm_router_body, geom=geom, n_exp=n_exp),
        grid=(n // tm,),
        in_specs=[pl.BlockSpec((tm, d), lambda i: (i, 0)),
                  pl.BlockSpec((1, d), lambda i: (l, 0)),
                  pl.BlockSpec(mod3.shape, lambda i: (0, 0, 0)),
                  pl.BlockSpec((d, n_exp), lambda i: (l, 0)),
                  pl.BlockSpec((1, n_exp), lambda i: (l, 0))],
        out_specs=[pl.BlockSpec((tm, d), lambda i: (i, 0)),
                   pl.BlockSpec((tm, LANES), lambda i: (i, 0)),
                   pl.BlockSpec((tm, LANES), lambda i: (i, 0)),
                   pl.BlockSpec((tm, LANES), lambda i: (i, 0)),
                   pl.BlockSpec((8, LANES), lambda i: (0, 0))],
        out_shape=[jax.ShapeDtypeStruct((n, d), F32),
                   jax.ShapeDtypeStruct((n, LANES), jnp.int32),
                   jax.ShapeDtypeStruct((n, LANES), F32),
                   jax.ShapeDtypeStruct((n, LANES), jnp.int32),
                   jax.ShapeDtypeStruct((8, LANES), F32)],
        scratch_shapes=[pltpu.VMEM((8, LANES), F32)],
        compiler_params=_cp(("arbitrary",)),
        name="norm_router",
    )(x, g, mod3, wr, br)


def _row_copy(src, dst, sem, s_row, d_slot):
    return pltpu.make_async_copy(src.at[pl.ds(s_row, 1), :], dst.at[pl.ds(d_slot, 1), :], sem)


DMA_UNROLL = 8


def _row_gather_step(idx_ref, src_ref, buf, sem, n_rows, n_steps, live_ref=None):
    i = pl.program_id(0)

    def groups(step):
        return (n_rows if live_ref is None else live_ref[step]) // DMA_UNROLL

    def issue(step, slot):
        def start(g, carry):
            for k in range(DMA_UNROLL):
                r = g * DMA_UNROLL + k
                _row_copy(src_ref, buf.at[slot], sem.at[slot], idx_ref[step * n_rows + r], r).start()
            return carry

        lax.fori_loop(0, groups(step), start, 0)

    @pl.when(i == 0)
    def _():
        issue(0, 0)

    @pl.when(i + 1 < n_steps)
    def _():
        issue(i + 1, (i + 1) % 2)

    slot = i % 2

    def wait(g, carry):
        for k in range(DMA_UNROLL):
            _row_copy(src_ref, buf.at[slot], sem.at[slot], 0, g * DMA_UNROLL + k).wait()
        return carry

    lax.fori_loop(0, groups(i), wait, 0)
    return slot


def _gather_body(rows_ref, live_ref, src_ref, o_ref, buf, sem, *, n_rows, n_steps):
    @pl.when(pl.program_id(0) == 0)
    def _():
        buf[...] = jnp.zeros(buf.shape, buf.dtype)

    slot = _row_gather_step(rows_ref, src_ref, buf, sem, n_rows, n_steps, live_ref)
    o_ref[...] = buf[slot].astype(o_ref.dtype)


def _moe_gather(rows, live, h2, tm):
    p_max = rows.shape[0]
    d = h2.shape[1]
    return pl.pallas_call(
        functools.partial(_gather_body, n_rows=tm, n_steps=p_max // tm),
        grid_spec=pltpu.PrefetchScalarGridSpec(
            num_scalar_prefetch=2,
            grid=(p_max // tm,),
            in_specs=[pl.BlockSpec(memory_space=pl.ANY)],
            out_specs=pl.BlockSpec((tm, d), lambda i, rows, live: (i, 0)),
            scratch_shapes=[pltpu.VMEM((2, tm, d), F32), pltpu.SemaphoreType.DMA((2,))]),
        out_shape=jax.ShapeDtypeStruct((p_max, d), BF16),
        compiler_params=_cp(("arbitrary",)),
        name="moe_gather",
    )(rows, live, h2)


def _expert_weights(plan, w_hbm, wbuf, wbf, sem, col_blocks, tn, e0, n_pass):
    blk_e, _, first, grp, e_next, meta, _ = plan
    j = pl.program_id(0)
    i = pl.program_id(1)
    n_grp = meta[1]

    def copies(e_idx, jj):
        return [pltpu.make_async_copy(w_hbm.at[e0 + e_idx, :, pl.ds(cb * tn, tn)], wbuf.at[p], sem.at[p])
                for p, cb in enumerate(col_blocks(jj))]

    @pl.when(first[i] == 1)
    def _():
        g = grp[i]

        @pl.when((j == 0) & (g == 0))
        def _():
            for c in copies(blk_e[0], 0):
                c.start()

        for c in copies(blk_e[i], j):
            c.wait()
        rows_per = 512 if wbf.shape[1] % 512 == 0 else wbf.shape[1]

        def cast_rows(r, carry):
            rows = pl.ds(pl.multiple_of(r * rows_per, rows_per), rows_per)
            for p in range(wbf.shape[0]):
                wbf[p, rows, :] = wbuf[p, rows, :].astype(BF16)
            return carry

        lax.fori_loop(0, wbf.shape[1] // rows_per, cast_rows, 0)
        last_in_pass = g == n_grp - 1

        @pl.when(jnp.logical_not(last_in_pass & (j == n_pass - 1)))
        def _():
            for c in copies(e_next[i], jnp.where(last_in_pass, j + 1, j)):
                c.start()


N_PLAN = 7


def _for_tile_rows(plan, o_ref, compute):
    i = pl.program_id(1)
    nv = plan[5][0]
    half = plan[6][i] == 1
    tm = o_ref.shape[0]

    @pl.when((i < nv) & jnp.logical_not(half))
    def _():
        compute(pl.ds(0, tm))

    @pl.when((i < nv) & half)
    def _():
        compute(pl.ds(0, tm // 2))
        o_ref[pl.ds(tm // 2, tm // 2), :] = jnp.zeros((tm // 2, o_ref.shape[1]), o_ref.dtype)

    @pl.when(i >= nv)
    def _():
        o_ref[...] = jnp.zeros(o_ref.shape, o_ref.dtype)


def _moe_gu_body(*refs, tn, e0, nj):
    plan, (x_ref, w_hbm, bg_ref, bu_ref, o_ref, wbuf, wbf, sem) = refs[:N_PLAN], refs[N_PLAN:]
    _expert_weights(plan, w_hbm, wbuf, wbf, sem, lambda jj: (jj, nj + jj), tn, e0, nj)

    def compute(rows):
        x = x_ref[rows, :]
        g = jnp.minimum(_dot(x, wbf[0]) + bg_ref[0], SWIGLU_LIMIT)
        u = jnp.clip(_dot(x, wbf[1]) + bu_ref[0], -SWIGLU_LIMIT, SWIGLU_LIMIT)
        o_ref[rows, :] = ((u + 1.0) * g * _sigmoid(SWIGLU_ALPHA * g)).astype(o_ref.dtype)

    _for_tile_rows(plan, o_ref, compute)


def _moe_gate_up(plan, xs, w, b, l, n_exp, tm):
    p_max, d = xs.shape
    de = w.shape[2] // 2
    tn = _pick(de, 512)
    nj = de // tn
    e0 = l * n_exp
    return pl.pallas_call(
        functools.partial(_moe_gu_body, tn=tn, e0=e0, nj=nj),
        grid_spec=pltpu.PrefetchScalarGridSpec(
            num_scalar_prefetch=N_PLAN,
            grid=(nj, p_max // tm),
            in_specs=[pl.BlockSpec((tm, d), lambda j, i, be, rb, *_: (rb[i], 0)),
                      pl.BlockSpec(memory_space=pl.ANY),
                      pl.BlockSpec((1, 1, tn), lambda j, i, be, *_: (e0 + be[i], 0, j)),
                      pl.BlockSpec((1, 1, tn), lambda j, i, be, *_: (e0 + be[i], 0, nj + j))],
            out_specs=pl.BlockSpec((tm, tn), lambda j, i, *_: (i, j)),
            scratch_shapes=[pltpu.VMEM((2, d, tn), F32), pltpu.VMEM((2, d, tn), BF16),
                            pltpu.SemaphoreType.DMA((2,))]),
        out_shape=jax.ShapeDtypeStruct((p_max, de), BF16),
        compiler_params=_cp(("arbitrary", "arbitrary")),
        name="moe_gate_up",
    )(*plan, xs, w, b, b)


def _moe_down_body(*refs, tn, e0, nj):
    plan, (x_ref, w_hbm, b_ref, o_ref, wbuf, wbf, sem) = refs[:N_PLAN], refs[N_PLAN:]
    _expert_weights(plan, w_hbm, wbuf, wbf, sem, lambda jj: (jj,), tn, e0, nj)

    def compute(rows):
        o_ref[rows, :] = _dot(x_ref[rows, :], wbf[0]) + b_ref[0]

    _for_tile_rows(plan, o_ref, compute)


def _moe_down(plan, act, w, b, l, n_exp, tm):
    p_max, de = act.shape
    d = w.shape[2]
    tn = _pick(d, 1024)
    nj = d // tn
    e0 = l * n_exp
    return pl.pallas_call(
        functools.partial(_moe_down_body, tn=tn, e0=e0, nj=nj),
        grid_spec=pltpu.PrefetchScalarGridSpec(
            num_scalar_prefetch=N_PLAN,
            grid=(nj, p_max // tm),
            in_specs=[pl.BlockSpec((tm, de), lambda j, i, be, rb, *_: (rb[i], 0)),
                      pl.BlockSpec(memory_space=pl.ANY),
                      pl.BlockSpec((1, 1, tn), lambda j, i, be, *_: (e0 + be[i], 0, j))],
            out_specs=pl.BlockSpec((tm, tn), lambda j, i, *_: (i, j)),
            scratch_shapes=[pltpu.VMEM((1, de, tn), F32), pltpu.VMEM((1, de, tn), BF16),
                            pltpu.SemaphoreType.DMA((1,))]),
        out_shape=jax.ShapeDtypeStruct((p_max, d), F32),
        compiler_params=_cp(("arbitrary", "arbitrary")),
        name="moe_down",
    )(*plan, act, w, b)


def _combine_body(dest_ref, yb_ref, x_ref, p_ref, mod_ref, op_ref, os_ref, buf, sem, *, geom, n_steps):
    i = pl.program_id(0)
    slot = _row_gather_step(dest_ref, yb_ref, buf, sem, ROW_BLOCK * TOP_K, n_steps)
    b = geom.bidx(i)
    p = p_ref[...]
    acc = p[:, 0:1] * buf[slot, pl.ds(0, ROW_BLOCK), :]
    for k in range(1, TOP_K):
        acc = acc + p[:, k:k + 1] * buf[slot, pl.ds(k * ROW_BLOCK, ROW_BLOCK), :]
    val = x_ref[...] + mod_ref[b, pl.ds(5, 1), :] * acc
    n_pb = geom.n_p // ROW_BLOCK

    @pl.when(i < n_pb)
    def _():
        op_ref[...] = val

    @pl.when(i >= n_pb)
    def _():
        os_ref[...] = val


def _moe_combine(dest_km, yb, x1, probs, mod3, geom):
    n, d = x1.shape
    n_pb = geom.n_p // ROW_BLOCK
    return pl.pallas_call(
        functools.partial(_combine_body, geom=geom, n_steps=n // ROW_BLOCK),
        grid_spec=pltpu.PrefetchScalarGridSpec(
            num_scalar_prefetch=1,
            grid=(n // ROW_BLOCK,),
            in_specs=[pl.BlockSpec(memory_space=pl.ANY),
                      pl.BlockSpec((ROW_BLOCK, d), lambda i, dest: (i, 0)),
                      pl.BlockSpec((ROW_BLOCK, LANES), lambda i, dest: (i, 0)),
                      pl.BlockSpec(mod3.shape, lambda i, dest: (0, 0, 0))],
            out_specs=[pl.BlockSpec((ROW_BLOCK, d), lambda i, dest: (jnp.minimum(i, n_pb - 1), 0)),
                       pl.BlockSpec((ROW_BLOCK, d), lambda i, dest: (jnp.maximum(i - n_pb, 0), 0))],
            scratch_shapes=[pltpu.VMEM((2, ROW_BLOCK * TOP_K, d), F32), pltpu.SemaphoreType.DMA((2,))]),
        out_shape=[jax.ShapeDtypeStruct((geom.n_p, d), F32), jax.ShapeDtypeStruct((geom.n_s, d), F32)],
        compiler_params=_cp(("arbitrary",)),
        name="moe_combine",
    )(dest_km, yb, x1, probs, mod3)


def _moe_plan(top_idx, rank, counts, n_exp, tm, gather_rows):
    n_tok = top_idx.shape[0]
    n_slot = n_tok * TOP_K
    flat_e = top_idx.reshape(n_slot)
    counts = counts.astype(jnp.int32)
    p_counts = (counts + tm - 1) // tm * tm
    p_ends = jnp.cumsum(p_counts)
    p_starts = p_ends - p_counts
    onehot = flat_e[:, None] == jnp.arange(n_exp, dtype=jnp.int32)[None, :]
    dest = (jnp.sum(jnp.where(onehot, p_starts[None, :], 0), axis=1) + rank.reshape(n_slot)).astype(jnp.int32)
    n_tiles = -(-n_slot // tm) + n_exp
    p_max = n_tiles * tm
    flat_tok = jnp.arange(n_slot, dtype=jnp.int32) // TOP_K
    rows = jnp.zeros((p_max,), jnp.int32).at[dest].set(flat_tok, unique_indices=True)
    tile_start = jnp.arange(n_tiles, dtype=jnp.int32) * tm
    nv = (p_ends[-1] // tm).astype(jnp.int32)
    blk_e = jnp.minimum(jnp.searchsorted(p_ends, tile_start, side="right"), n_exp - 1).astype(jnp.int32)
    last_e = blk_e[jnp.maximum(nv - 1, 0)]
    tile_id = jnp.arange(n_tiles, dtype=jnp.int32)
    blk_e = jnp.where(tile_id < nv, blk_e, last_e)
    row_blk = jnp.minimum(tile_id, jnp.maximum(nv - 1, 0))
    first = (tile_id < nv) & ((tile_id == 0) | (blk_e != jnp.roll(blk_e, 1)))
    grp = jnp.cumsum(first.astype(jnp.int32)) - 1
    n_grp = jnp.sum(first.astype(jnp.int32))
    starts = jnp.sort(jnp.where(first, tile_id, n_tiles))
    grp_e = blk_e[jnp.minimum(starts, n_tiles - 1)]
    e_next = grp_e[(grp + 1) % n_grp]
    meta = jnp.stack([nv, n_grp]).astype(jnp.int32)
    live = jnp.minimum(counts[blk_e] - (tile_start - p_starts[blk_e]), tm)
    half = ((tile_id < nv) & (live <= tm // 2)).astype(jnp.int32)
    plan = (blk_e, row_blk, first.astype(jnp.int32), grp.astype(jnp.int32), e_next.astype(jnp.int32), meta, half)
    dest_km = dest.reshape(n_tok // ROW_BLOCK, ROW_BLOCK, TOP_K).transpose(0, 2, 1).reshape(n_slot)
    per_tile = tm // gather_rows
    step_id = jnp.arange(n_tiles * per_tile, dtype=jnp.int32)
    tile_live = jnp.where(tile_id < nv, live, 0)
    glive = jnp.clip(tile_live[step_id // per_tile] - (step_id % per_tile) * gather_rows, 0, gather_rows)
    glive = ((glive + DMA_UNROLL - 1) // DMA_UNROLL * DMA_UNROLL).astype(jnp.int32)
    return rows, glive, dest_km, plan


def kernel(x_prompt, x_sample, cache_k, cache_v, state_ssm_re, state_ssm_im, c_prompt, c_sample,
           w_ada, b_ada, g_norm_mix, g_norm_ffn, w_in, b_in, g_q, g_k, w_attn_up,
           ssm_a_re, ssm_a_im, ssm_log_dt, ssm_b_re, ssm_b_im, ssm_c_re, ssm_c_im, ssm_d,
           w_glu, w_out, w_router, b_router, w_gate_up, b_gate_up, w_down, b_down):
    B, T, D = x_prompt.shape
    DB, DT, _ = x_sample.shape
    depth = w_ada.shape[0]
    H, hd = cache_k.shape[3], cache_k.shape[4]
    AW = H * hd
    G, N, P = ssm_b_re.shape[1:]
    SW = G * P
    E = w_router.shape[2]
    assert hd == LANES and P == 16 and 2 * N == LANES and DT == ROW_BLOCK
    geom = _Geom(B, T, DB, DT)
    n_p, n_tok, nbp = geom.n_p, geom.n_tok, geom.nbp
    L = SSM_CHUNK
    assert T % L == 0 and DT % L == 0
    moe_tm = 512
    gather_rows = 256

    xp, xs_ = x_prompt.reshape(n_p, D), x_sample.reshape(geom.n_s, D)
    c_all = jnp.concatenate([c_prompt, c_sample, jnp.zeros((nbp - geom.nb, D), F32)], axis=0)

    flat2 = lambda w: w.reshape((w.shape[0] * w.shape[1],) + w.shape[2:])
    w_ada2, w_in2, w_au2, w_glu2, w_out2, w_r2 = map(flat2, (w_ada, w_in, w_attn_up, w_glu, w_out, w_router))
    w_gu3, w_dn3 = flat2(w_gate_up), flat2(w_down)
    b_gu3 = b_gate_up.reshape(depth * E, 1, b_gate_up.shape[2])
    b_dn3 = b_down.reshape(depth * E, 1, b_down.shape[2])

    NG = LANES // P
    assert G % NG == 0
    GB, NS = G // NG, NG * N
    lam = jnp.stack([ssm_a_re.reshape(depth * GB, NS), ssm_a_im.reshape(depth * GB, NS)], axis=1)
    ldt = jnp.repeat(ssm_log_dt, N, axis=-1).reshape(depth * GB, 1, NS)
    bb = jnp.stack([ssm_b_re.reshape(depth * GB, NS, P), ssm_b_im.reshape(depth * GB, NS, P)], axis=1)
    cb = jnp.stack([ssm_c_re.reshape(depth * GB, LANES, N), ssm_c_im.reshape(depth * GB, LANES, N)], axis=1)
    db = ssm_d.reshape(depth * GB, 1, LANES)

    nc_p, nc_s = n_p // L, geom.n_s // L
    NC = nc_p + nc_s
    cps, css = T // L, DT // L
    lidx = jnp.concatenate([jnp.arange(nc_p, dtype=jnp.int32) % cps,
                            jnp.arange(nc_s, dtype=jnp.int32) % css]).reshape(NC, 1)
    n_steps = max(cps - 1, css - 1, 1).bit_length()
    last_rows = tuple((b + 1) * cps - 1 for b in range(B)) + tuple(nc_p + (b + 1) * css - 1 for b in range(DB))
    first_rows = tuple(nc_p + b * css for b in range(DB))

    ks, vs, srs, sis = [], [], [], []
    for l in range(depth):
        mod = _ada(c_all, w_ada2, b_ada, l)
        mod3 = mod.reshape(nbp, 6, D)
        h = _norm_mod(xp, xs_, g_norm_mix, mod3, geom, l, 0, 1)

        dn = functools.partial(_dense, h, w_in2, l, geom=geom, bias=b_in)
        split_f32 = ((F32, "p"), (F32, "s"), (BF16, "all"))
        (qb,) = dn(0, AW, mode="q", extra=(g_q,), extra_kinds=("row",), out_kinds=((BF16, "all"),), name="proj_q")
        kfp, kfs, kb = dn(AW, AW, mode="k", extra=(g_k,), extra_kinds=("row",), out_kinds=split_f32, name="proj_k")
        vfp, vfs, vb = dn(2 * AW, AW, mode="v", out_kinds=split_f32, name="proj_v")
        (u,) = dn(3 * AW, SW, mode="plain", name="proj_u")
        (sga,) = dn(3 * AW + SW, D, mode="sig", name="proj_ga")
        (sgb,) = dn(3 * AW + SW + D, D, mode="sig", name="proj_gb")

        attn = jnp.concatenate([_sb_prompt(qb, kb, vb, B, T, H, hd),
                                _sb_sample(qb, kb, vb, cache_k, cache_v, l, DB, DT, H, hd, n_p)], axis=0)
        (ma,) = _dense(attn, w_au2, l, 0, D, mode="mul", geom=geom, extra=(sga,), extra_kinds=("tile",),
                       name="attn_up")

        x0 = jnp.concatenate([state_ssm_re[l].reshape(DB, GB, NS), state_ssm_im[l].reshape(DB, GB, NS)],
                             axis=-1).transpose(1, 0, 2)
        yg, fin = _s5(u, x0, lidx, lam, ldt, bb, cb, db, l, N, n_steps, first_rows, last_rows)
        fin = fin.reshape(GB, geom.nb, 2, NG, N).transpose(2, 1, 0, 3, 4).reshape(2, geom.nb, G, N)

        merged = _glu_merge(yg, w_glu2, l, ma, sgb, geom)
        (x1,) = _dense(merged, w_out2, l, 0, D, mode="resid", geom=geom, extra=(xp, xs_, mod3),
                       extra_kinds=("tile_p", "tile_s", "mod"), name="out_proj")

        h2, top_idx, probs, rank, cnt = _norm_router(x1, g_norm_ffn, mod3, w_r2, b_router, geom, l)
        rows, glive, dest_km, plan = _moe_plan(top_idx[:, :TOP_K], rank[:, :TOP_K], cnt[0, :E], E, moe_tm,
                                               gather_rows)
        xs = _moe_gather(rows, glive, h2, gather_rows)
        act = _moe_gate_up(plan, xs, w_gu3, b_gu3, l, E, moe_tm)
        yb = _moe_down(plan, act, w_dn3, b_dn3, l, E, moe_tm)
        xp, xs_ = _moe_combine(dest_km, yb, x1, probs, mod3, geom)

        ks.append((kfp, kfs))
        vs.append((vfp, vfs))
        srs.append(fin[0])
        sis.append(fin[1])

    def stacked(pairs, part, lead):
        return jnp.stack([p[part] for p in pairs]).reshape((depth,) + lead + (H, hd))

    sr, si = jnp.stack(srs), jnp.stack(sis)
    return (xp.reshape(B, T, D), xs_.reshape(DB, DT, D), stacked(ks, 0, (B, T)), stacked(vs, 0, (B, T)),
            sr[:, :B], si[:, :B], stacked(ks, 1, (DB, DT)), stacked(vs, 1, (DB, DT)), sr[:, B:], si[:, B:])
```

```python
import functools
import math

import jax
import jax.numpy as jnp
from jax import lax
from jax.experimental import pallas as pl
from jax.experimental.pallas import tpu as pltpu

F32 = jnp.float32
BF16 = jnp.bfloat16

NORM_EPS = 1e-6
TOP_K = 4
SWIGLU_LIMIT = 7.0
SWIGLU_ALPHA = 1.702
ROW_BLOCK = 64
SSM_CHUNK = 16
LANES = 128
VMEM_LIMIT = 60 * 1024 * 1024


def _cp(sem, vmem=VMEM_LIMIT):
    return pltpu.CompilerParams(dimension_semantics=sem, vmem_limit_bytes=vmem)


def _pick(n, pref):
    t = pref
    while t > 1 and n % t:
        t //= 2
    return t


def _sigmoid(x):
    return 1.0 / (1.0 + jnp.exp(-x))


def _dot(a, b):
    return jnp.dot(a, b, preferred_element_type=F32)


def _split(a):
    hi = a.astype(BF16)
    lo = (a - hi.astype(F32)).astype(BF16)
    return hi, lo


def _dot3(a, b):
    ah, al = _split(a)
    bh, bl = _split(b)
    return _dot(ah, bh) + (_dot(al, bh) + _dot(ah, bl))


class _Geom:
    def __init__(self, B, T, DB, DT):
        self.B, self.T, self.DB, self.DT = B, T, DB, DT
        self.n_p, self.n_s = B * T, DB * DT
        self.n_tok = self.n_p + self.n_s
        assert T % ROW_BLOCK == 0 and DT % ROW_BLOCK == 0
        self.tm = 512
        while self.n_p % self.tm or self.n_s % self.tm:
            self.tm //= 2
        assert self.tm >= ROW_BLOCK
        self.sub = self.tm // ROW_BLOCK
        self.npt = self.n_p // self.tm
        self.nb = B + DB
        self.nbp = -(-self.nb // 8) * 8

    def p_blk(self, i):
        return jnp.minimum(i, self.npt - 1)

    def s_blk(self, i):
        return jnp.maximum(i - self.npt, 0)

    def bidx(self, blk):
        pb = self.n_p // ROW_BLOCK
        return jnp.where(blk < pb, blk // (self.T // ROW_BLOCK),
                         self.B + (blk - pb) // (self.DT // ROW_BLOCK))


def _ada_body(c_ref, w_ref, b_ref, o_ref):
    c = c_ref[...]
    s = c * _sigmoid(c)
    o_ref[...] = _dot(s.astype(BF16), w_ref[...].astype(BF16)) + b_ref[...]


def _ada(c_pad, w, b, l):
    nbp, d = c_pad.shape
    n = w.shape[1]
    tn = _pick(n, 1024)
    return pl.pallas_call(
        _ada_body,
        grid=(n // tn,),
        in_specs=[pl.BlockSpec((nbp, d), lambda j: (0, 0)),
                  pl.BlockSpec((d, tn), lambda j: (l, j)),
                  pl.BlockSpec((1, tn), lambda j: (l, j))],
        out_specs=pl.BlockSpec((nbp, tn), lambda j: (0, j)),
        out_shape=jax.ShapeDtypeStruct((nbp, n), F32),
        compiler_params=_cp(("arbitrary",)),
        name="ada_mod",
    )(c_pad, w, b)


def _norm_rows(x, g, sc, sh):
    ms = jnp.mean(x * x, axis=-1, keepdims=True)
    return (x * lax.rsqrt(ms + NORM_EPS)) * g * (1.0 + sc) + sh


def _load_rows(i, geom, ref_p, ref_s, rows):
    return jnp.where(i < geom.npt, ref_p[rows, :], ref_s[rows, :])


def _store_tile(i, geom, ref_p, ref_s, val):
    @pl.when(i < geom.npt)
    def _():
        ref_p[...] = val

    @pl.when(i >= geom.npt)
    def _():
        ref_s[...] = val


def _norm_mod_body(xp_ref, xs_ref, g_ref, mod_ref, o_ref, *, geom, i_sh, i_sc):
    i = pl.program_id(0)
    for r in range(geom.sub):
        b = geom.bidx(i * geom.sub + r)
        rows = pl.ds(r * ROW_BLOCK, ROW_BLOCK)
        h = _norm_rows(_load_rows(i, geom, xp_ref, xs_ref, rows), g_ref[...], mod_ref[b, pl.ds(i_sc, 1), :],
                       mod_ref[b, pl.ds(i_sh, 1), :])
        o_ref[rows, :] = h.astype(o_ref.dtype)


def _norm_mod(xp, xs, g, mod3, geom, l, i_sh, i_sc):
    d = xp.shape[1]
    tm = geom.tm
    return pl.pallas_call(
        functools.partial(_norm_mod_body, geom=geom, i_sh=i_sh, i_sc=i_sc),
        grid=(geom.n_tok // tm,),
        in_specs=[pl.BlockSpec((tm, d), lambda i: (geom.p_blk(i), 0)),
                  pl.BlockSpec((tm, d), lambda i: (geom.s_blk(i), 0)),
                  pl.BlockSpec((1, d), lambda i: (l, 0)),
                  pl.BlockSpec(mod3.shape, lambda i: (0, 0, 0))],
        out_specs=pl.BlockSpec((tm, d), lambda i: (i, 0)),
        out_shape=jax.ShapeDtypeStruct((geom.n_tok, d), BF16),
        compiler_params=_cp(("arbitrary",)),
        name="norm_mod",
    )(xp, xs, g, mod3)


def _head_norm(acc, g):
    outs = []
    for c in range(acc.shape[1] // LANES):
        a = acc[:, c * LANES:(c + 1) * LANES]
        ms = jnp.mean(a * a, axis=-1, keepdims=True)
        outs.append(a * lax.rsqrt(ms + NORM_EPS) * g)
    return jnp.concatenate(outs, axis=1)


def _dense_body(*refs, mode, geom, n_extra, n_out, has_bias):
    x_ref, w_ref = refs[0], refs[1]
    pos = 2
    b_ref = None
    if has_bias:
        b_ref = refs[pos]
        pos += 1
    extra = refs[pos:pos + n_extra]
    outs = refs[pos + n_extra:pos + n_extra + n_out]
    wb = refs[pos + n_extra + n_out]
    i = pl.program_id(1)

    @pl.when(i == 0)
    def _():
        wb[...] = w_ref[...].astype(BF16)

    acc = _dot(x_ref[...], wb[...])
    if has_bias:
        acc = acc + b_ref[...]
    if mode == "q":
        qscale = LANES ** -0.5 * LOG2_E
        outs[0][...] = (_head_norm(acc, extra[0][...]) * qscale).astype(BF16)
    elif mode == "k":
        a = _head_norm(acc, extra[0][...])
        _store_tile(i, geom, outs[0], outs[1], a)
        outs[2][...] = a.astype(BF16)
    elif mode == "v":
        _store_tile(i, geom, outs[0], outs[1], acc)
        outs[2][...] = acc.astype(BF16)
    elif mode == "plain":
        outs[0][...] = acc
    elif mode == "sig":
        outs[0][...] = _sigmoid(acc)
    elif mode == "mul":
        outs[0][...] = acc * extra[0][...]
    elif mode == "resid":
        xres_p, xres_s, mod_ref = extra
        for r in range(geom.sub):
            b = geom.bidx(i * geom.sub + r)
            rows = pl.ds(r * ROW_BLOCK, ROW_BLOCK)
            gate = mod_ref[b, pl.ds(2, 1), :]
            outs[0][rows, :] = (_load_rows(i, geom, xres_p, xres_s, rows)
                                + gate * acc[r * ROW_BLOCK:(r + 1) * ROW_BLOCK, :])
    else:
        raise ValueError(mode)


def _dense(x, w, l, col0, ncols, *, mode, geom, bias=None, extra=(), extra_kinds=(), out_kinds=((F32, "all"),),
           tn_pref=1024, name="dense"):
    m, k = x.shape
    tm = geom.tm
    tn = _pick(math.gcd(ncols, col0) if col0 else ncols, tn_pref)
    cb = col0 // tn
    row_maps = {"all": lambda j, i: (i, j), "p": lambda j, i: (geom.p_blk(i), j),
                "s": lambda j, i: (geom.s_blk(i), j)}
    n_rows = {"all": m, "p": geom.n_p, "s": geom.n_s}
    in_specs = [pl.BlockSpec((tm, k), lambda j, i: (i, 0)),
                pl.BlockSpec((k, tn), lambda j, i: (l, cb + j), pipeline_mode=pl.Buffered(1))]
    args = [x, w]
    if bias is not None:
        in_specs.append(pl.BlockSpec((1, tn), lambda j, i: (l, cb + j)))
        args.append(bias)
    for a, kind in zip(extra, extra_kinds):
        if kind == "tile":
            in_specs.append(pl.BlockSpec((tm, tn), row_maps["all"]))
        elif kind in ("tile_p", "tile_s"):
            in_specs.append(pl.BlockSpec((tm, tn), row_maps[kind[-1]]))
        elif kind == "row":
            in_specs.append(pl.BlockSpec((1, a.shape[1]), lambda j, i: (l, 0)))
        elif kind == "mod":
            in_specs.append(pl.BlockSpec((a.shape[0], a.shape[1], tn), lambda j, i: (0, 0, j)))
        args.append(a)
    out_specs = [pl.BlockSpec((tm, tn), row_maps[part]) for _, part in out_kinds]
    out_shape = [jax.ShapeDtypeStruct((n_rows[part], ncols), dt) for dt, part in out_kinds]
    res = pl.pallas_call(
        functools.partial(_dense_body, mode=mode, geom=geom, n_extra=len(extra), n_out=len(out_kinds),
                          has_bias=bias is not None),
        grid=(ncols // tn, m // tm),
        in_specs=in_specs,
        out_specs=out_specs,
        out_shape=out_shape,
        scratch_shapes=[pltpu.VMEM((k, tn), BF16)],
        compiler_params=_cp(("arbitrary", "arbitrary")),
        name=name,
    )(*args)
    return res


def _glu_body(x_ref, wv_ref, wg_ref, ma_ref, sgb_ref, o_ref, wvb, wgb):
    @pl.when(pl.program_id(1) == 0)
    def _():
        wvb[...] = wv_ref[...].astype(BF16)
        wgb[...] = wg_ref[...].astype(BF16)

    x = x_ref[...].astype(BF16)
    val = _dot(x, wvb[...])
    gate = _dot(x, wgb[...])
    o_ref[...] = (ma_ref[...] + sgb_ref[...] * (val * _sigmoid(gate))).astype(o_ref.dtype)


def _glu_merge(yg, w, l, ma, sgb, geom):
    m, k = yg.shape
    d = ma.shape[1]
    tm = geom.tm
    tn = _pick(d, 1024)
    nj = d // tn
    return pl.pallas_call(
        _glu_body,
        grid=(nj, m // tm),
        in_specs=[pl.BlockSpec((tm, k), lambda j, i: (i, 0)),
                  pl.BlockSpec((k, tn), lambda j, i: (l, j), pipeline_mode=pl.Buffered(1)),
                  pl.BlockSpec((k, tn), lambda j, i: (l, nj + j), pipeline_mode=pl.Buffered(1)),
                  pl.BlockSpec((tm, tn), lambda j, i: (i, j)),
                  pl.BlockSpec((tm, tn), lambda j, i: (i, j))],
        out_specs=pl.BlockSpec((tm, tn), lambda j, i: (i, j)),
        out_shape=jax.ShapeDtypeStruct((m, d), BF16),
        scratch_shapes=[pltpu.VMEM((k, tn), BF16), pltpu.VMEM((k, tn), BF16)],
        compiler_params=_cp(("arbitrary", "arbitrary")),
        name="glu_merge",
    )(yg, w, w, ma, sgb)


LOG2_E = 1.4426950408889634


def _neg_abs(z):
    return lax.bitcast_convert_type(lax.bitcast_convert_type(z, jnp.uint32) | jnp.uint32(0x80000000), F32)


def _sb_tile(q, kt, vt, tri, mask, c, o):
    z = lax.dot_general(q, kt, (((1,), (1,)), ((), ())), preferred_element_type=F32)
    sp = jnp.maximum(z, 0.0) + jnp.log2(1.0 + jnp.exp2(_neg_abs(z)))
    if mask is not None:
        sp = jnp.where(mask, sp, 0.0)
    incl = _dot(sp.astype(BF16), tri)
    w = jnp.exp2(z - incl - c)
    if mask is not None:
        w = jnp.where(mask, w, 0.0)
    o = o + _dot(w.astype(BF16), vt)
    c = c + incl[:, 0:1]
    return c, o


def _suffix_matrix(tk):
    j = lax.broadcasted_iota(jnp.int32, (tk, tk), 0)
    s = lax.broadcasted_iota(jnp.int32, (tk, tk), 1)
    return (j >= s).astype(BF16)


def _sb_prompt_body(q_ref, k_ref, v_ref, o_ref, *, tq, tk):
    qi = pl.program_id(2)
    q = q_ref[...]
    r = tq // tk
    tri = _suffix_matrix(tk)
    row = lax.broadcasted_iota(jnp.int32, (tq, tk), 0)
    col = lax.broadcasted_iota(jnp.int32, (tq, tk), 1)
    c = jnp.zeros((tq, 1), F32)
    o = jnp.zeros((tq, q.shape[1]), F32)
    for d in reversed(range(r)):
        ks = pl.ds(pl.multiple_of((qi * r + d) * tk, tk), tk)
        c, o = _sb_tile(q, k_ref[ks, :], v_ref[ks, :], tri, col + d * tk < row, c, o)

    def body(it, carry):
        c, o = carry
        for u in range(r):
            ks = pl.ds(pl.multiple_of((qi * r - 1 - (it * r + u)) * tk, tk), tk)
            c, o = _sb_tile(q, k_ref[ks, :], v_ref[ks, :], tri, None, c, o)
        return c, o

    c, o = lax.fori_loop(0, qi, body, (c, o))
    o_ref[...] = o.astype(o_ref.dtype)


def _sb_prompt(q, kb, vb, B, T, H, hd):
    tq = _pick(T, 512)
    tk = _pick(tq, 256)
    nq = T // tq
    return pl.pallas_call(
        functools.partial(_sb_prompt_body, tq=tq, tk=tk),
        grid=(B, H, nq),
        in_specs=[pl.BlockSpec((tq, hd), lambda b, h, i: (b * nq + i, h)),
                  pl.BlockSpec((T, hd), lambda b, h, i: (b, h)),
                  pl.BlockSpec((T, hd), lambda b, h, i: (b, h))],
        out_specs=pl.BlockSpec((tq, hd), lambda b, h, i: (b * nq + i, h)),
        out_shape=jax.ShapeDtypeStruct((B * T, H * hd), BF16),
        compiler_params=_cp(("arbitrary", "arbitrary", "arbitrary")),
        name="sb_prompt",
    )(q, kb, vb)


def _sb_sample_body(q_ref, k_ref, v_ref, ck_ref, cv_ref, o_ref, *, dt, tk, past, hb, hd):
    row = lax.broadcasted_iota(jnp.int32, (dt, dt), 0)
    col = lax.broadcasted_iota(jnp.int32, (dt, dt), 1)
    tri_new = _suffix_matrix(dt)
    tri = _suffix_matrix(tk)
    n_t = past // tk
    heads = [slice(h * hd, (h + 1) * hd) for h in range(hb)]
    c0 = jnp.zeros((dt, 1), F32)
    o0 = jnp.zeros((dt, hd), F32)
    carry = tuple(_sb_tile(q_ref[:, sl], k_ref[:, sl], v_ref[:, sl], tri_new, col < row, c0, o0)
                  for sl in heads)

    def body(it, carry):
        ks = pl.ds(pl.multiple_of((n_t - 1 - it) * tk, tk), tk)
        return tuple(_sb_tile(q_ref[:, sl], ck_ref[0, 0, ks, h, :].astype(BF16), cv_ref[0, 0, ks, h, :].astype(BF16),
                              tri, None, co[0], co[1])
                     for h, (sl, co) in enumerate(zip(heads, carry)))

    carry = lax.fori_loop(0, n_t, body, carry)
    for sl, co in zip(heads, carry):
        o_ref[:, sl] = co[1].astype(o_ref.dtype)


def _sb_sample(q, kb, vb, ck, cv, l, DB, DT, H, hd, n_p):
    past = ck.shape[2]
    tk = _pick(past, 256)
    hb = _pick(H, 8)
    rb0 = n_p // DT
    new = lambda b, h: (rb0 + b, h)
    return pl.pallas_call(
        functools.partial(_sb_sample_body, dt=DT, tk=tk, past=past, hb=hb, hd=hd),
        grid=(DB, H // hb),
        in_specs=[pl.BlockSpec((DT, hb * hd), new),
                  pl.BlockSpec((DT, hb * hd), new),
                  pl.BlockSpec((DT, hb * hd), new),
                  pl.BlockSpec((1, 1, past, hb, hd), lambda b, h: (l, b, 0, h, 0)),
                  pl.BlockSpec((1, 1, past, hb, hd), lambda b, h: (l, b, 0, h, 0))],
        out_specs=pl.BlockSpec((DT, hb * hd), lambda b, h: (b, h)),
        out_shape=jax.ShapeDtypeStruct((DB * DT, H * hd), BF16),
        compiler_params=_cp(("arbitrary", "arbitrary")),
        name="sb_sample",
    )(q, kb, vb, ck, cv)


def _cexp(lr, li, t):
    mag = jnp.exp(lr * t)
    return mag * jnp.cos(li * t), mag * jnp.sin(li * t)


def _split3(a):
    hi = a.astype(BF16)
    r1 = a - hi.astype(F32)
    mid = r1.astype(BF16)
    return hi, mid, (r1 - mid.astype(F32)).astype(BF16)


NT_DIMS = (((1,), (1,)), ((), ()))


def _s5_body(u_ref, x0_ref, lidx_ref, lam_ref, ldt_ref, b_ref, c_ref, d_ref, y_ref, fin_ref,
             wst, vst, tbig, ucat, x0s, e_scr, *, n_state, n_steps, first_rows, last_rows):
    L, P, N = SSM_CHUNK, 16, n_state
    NS = (LANES // P) * N
    nc = ucat.shape[0]
    lr, li = lam_ref[0, 0:1, :], lam_ref[0, 1:2, :]
    dt = jnp.exp(ldt_ref[0])
    abr, abi = _cexp(lr, li, dt)
    den = lr * lr + li * li
    nr = abr - 1.0
    cor = (nr * lr + abi * li) / den
    coi = (abi * lr - nr * li) / den

    expand = (lax.broadcasted_iota(jnp.int32, (N, NS), 1) % N
              == lax.broadcasted_iota(jnp.int32, (N, NS), 0)).astype(BF16)
    own = (lax.broadcasted_iota(jnp.int32, (LANES, NS), 0) // P
           == lax.broadcasted_iota(jnp.int32, (LANES, NS), 1) // N)

    def block_diag(x):
        h, m, lo = _split3(x)
        return jnp.where(own, _dot(h, expand) + (_dot(m, expand) + _dot(lo, expand)), 0.0)

    spread = (lax.broadcasted_iota(jnp.int32, (LANES, P), 0) % P
              == lax.broadcasted_iota(jnp.int32, (LANES, P), 1)).astype(BF16)

    def block_diag_t(x):
        h, m, lo = _split3(x)
        nt = lambda a: lax.dot_general(spread, a, NT_DIMS, preferred_element_type=F32)
        return jnp.where(own, nt(h) + (nt(m) + nt(lo)), 0.0)

    br, bi = block_diag_t(b_ref[0, 0]), block_diag_t(b_ref[0, 1])
    cr, ci = block_diag(c_ref[0, 0]), block_diag(c_ref[0, 1])
    wr, wi = cor * br - coi * bi, cor * bi + coi * br
    vr, vi = cr * abr - ci * abi, cr * abi + ci * abr
    for m in range(L):
        wst[pl.ds((L - 1 - m) * LANES, LANES), :] = jnp.concatenate([wr, wi], axis=1).astype(BF16)
        vst[pl.ds(m * LANES, LANES), :] = jnp.concatenate([vr, -vi], axis=1).astype(BF16)
        if m + 1 < L:
            wr, wi = wr * abr - wi * abi, wr * abi + wi * abr
            vr, vi = vr * abr - vi * abi, vr * abi + vi * abr

    cc = jnp.concatenate([cr, -ci], axis=1).astype(BF16)
    t_all = lax.dot_general(wst[...], cc, NT_DIMS, preferred_element_type=F32)
    eye = (lax.broadcasted_iota(jnp.int32, (LANES, LANES), 0)
           == lax.broadcasted_iota(jnp.int32, (LANES, LANES), 1))
    t0 = t_all[(L - 1) * LANES:, :] + jnp.where(eye, d_ref[0], 0.0)
    t_bf = jnp.concatenate([t_all[:(L - 1) * LANES, :], t0], axis=0).astype(BF16)
    for i in range(L):
        cols = pl.ds(i * LANES, LANES)
        tbig[pl.ds(0, (i + 1) * LANES), cols] = t_bf[(L - 1 - i) * LANES:, :]
        if i + 1 < L:
            tbig[pl.ds((i + 1) * LANES, (L - 1 - i) * LANES), cols] = jnp.zeros(((L - 1 - i) * LANES, LANES), BF16)

    for j in range(L):
        ucat[:, pl.ds(j * LANES, LANES)] = u_ref[pl.ds(j, nc, stride=L), :].astype(BF16)
    uc = ucat[...]
    s_loc = _dot(uc, wst[...])

    x0s[...] = jnp.zeros(x0s.shape, F32)
    for b, row in enumerate(first_rows):
        x0s[pl.ds(row, 1), :] = x0_ref[0, pl.ds(b, 1), :]
    x0 = x0s[...]

    def cmul(t, s):
        pr, pi = _cexp(lr, li, t * dt)
        sr, si = s[:, :NS], s[:, NS:]
        return jnp.concatenate([pr * sr - pi * si, pr * si + pi * sr], axis=1)

    lidx = lidx_ref[...]
    e = s_loc + cmul(float(L), x0)
    for kk in range(n_steps):
        sh = pltpu.roll(e, 1 << kk, axis=0)
        e = e + jnp.where(lidx >= (1 << kk), cmul(float(L * (1 << kk)), sh), 0.0)
    s_in = jnp.where(lidx >= 1, pltpu.roll(e, 1, axis=0), x0)
    y = _dot(uc, tbig[...]) + lax.dot_general(s_in.astype(BF16), vst[...], NT_DIMS, preferred_element_type=F32)
    g = 0.5 * y * (1.0 + lax.erf(y * (2.0 ** -0.5)))
    for i in range(L):
        y_ref[pl.ds(i, nc, stride=L), :] = g[:, i * LANES:(i + 1) * LANES]
    e_scr[...] = e
    for r, src in enumerate(last_rows):
        fin_ref[0, pl.ds(r, 1), :] = e_scr[pl.ds(src, 1), :]


def _s5(u, x0, lidx, lam, ldt, bb, cb, db, l, n_state, n_steps, first_rows, last_rows):
    n_tok, sw = u.shape
    gb = sw // LANES
    nc = n_tok // SSM_CHUNK
    ns2 = 2 * (LANES // 16) * n_state
    nb = len(last_rows)
    w = SSM_CHUNK * LANES
    blk3 = lambda g: (l * gb + g, 0, 0)
    blk4 = lambda g: (l * gb + g, 0, 0, 0)
    return pl.pallas_call(
        functools.partial(_s5_body, n_state=n_state, n_steps=n_steps, first_rows=first_rows, last_rows=last_rows),
        grid=(gb,),
        in_specs=[pl.BlockSpec((n_tok, LANES), lambda g: (0, g)),
                  pl.BlockSpec((1, x0.shape[1], ns2), lambda g: (g, 0, 0)),
                  pl.BlockSpec((nc, 1), lambda g: (0, 0)),
                  pl.BlockSpec((1, 2, ns2 // 2), blk3),
                  pl.BlockSpec((1, 1, ns2 // 2), blk3),
                  pl.BlockSpec((1, 2, ns2 // 2, 16), blk4),
                  pl.BlockSpec((1, 2, LANES, n_state), blk4),
                  pl.BlockSpec((1, 1, LANES), blk3)],
        out_specs=[pl.BlockSpec((n_tok, LANES), lambda g: (0, g)),
                   pl.BlockSpec((1, nb, ns2), lambda g: (g, 0, 0))],
        out_shape=[jax.ShapeDtypeStruct((n_tok, sw), F32),
                   jax.ShapeDtypeStruct((gb, nb, ns2), F32)],
        scratch_shapes=[pltpu.VMEM((w, ns2), BF16), pltpu.VMEM((w, ns2), BF16), pltpu.VMEM((w, w), BF16),
                        pltpu.VMEM((nc, w), BF16), pltpu.VMEM((nc, ns2), F32), pltpu.VMEM((nc, ns2), F32)],
        compiler_params=_cp(("arbitrary",)),
        name="s5_block",
    )(u, x0, lidx, lam, ldt, bb, cb, db)


RANK_BLOCK = 128


def _norm_router_body(x_ref, g_ref, mod_ref, wr_ref, br_ref, h_ref, idx_ref, p_ref, rank_ref, cnt_ref, base_scr,
                      *, geom, n_exp):
    i = pl.program_id(0)
    for r in range(geom.sub):
        b = geom.bidx(i * geom.sub + r)
        rows = pl.ds(r * ROW_BLOCK, ROW_BLOCK)
        h_ref[rows, :] = _norm_rows(x_ref[rows, :], g_ref[...], mod_ref[b, pl.ds(4, 1), :],
                                    mod_ref[b, pl.ds(3, 1), :])
    logits = _dot3(h_ref[...], wr_ref[...]) + br_ref[...]
    tm = logits.shape[0]
    lane = lax.broadcasted_iota(jnp.int32, (tm, n_exp), 1)
    wide = lax.broadcasted_iota(jnp.int32, (tm, LANES), 1)
    idx_out = jnp.zeros((tm, LANES), jnp.int32)
    vals, sels = [], []
    for k in range(TOP_K):
        m = jnp.max(logits, axis=-1, keepdims=True)
        sel = jnp.min(jnp.where(logits == m, lane, n_exp), axis=-1, keepdims=True)
        idx_out = jnp.where(wide == k, sel, idx_out)
        vals.append(m)
        sels.append(sel)
        logits = jnp.where(lane == sel, -jnp.inf, logits)
    es = [jnp.exp(v - vals[0]) for v in vals]
    tot = es[0] + es[1] + es[2] + es[3]
    p_out = jnp.zeros((tm, LANES), F32)
    for k in range(TOP_K):
        p_out = jnp.where(wide == k, es[k] / tot, p_out)
    idx_ref[...] = idx_out
    p_ref[...] = p_out

    @pl.when(i == 0)
    def _():
        base_scr[...] = jnp.zeros(base_scr.shape, F32)

    base = base_scr[0:1, :]
    rb = min(RANK_BLOCK, tm)
    ns = TOP_K * rb
    before = (lax.broadcasted_iota(jnp.int32, (ns, ns), 1)
              < lax.broadcasted_iota(jnp.int32, (ns, ns), 0)).astype(BF16)
    wide_rb = lax.broadcasted_iota(jnp.int32, (rb, LANES), 1)
    ranks = []
    for s in range(tm // rb):
        onehot = jnp.concatenate([(wide_rb == sel[s * rb:(s + 1) * rb]).astype(F32) for sel in sels], axis=0)
        earlier = _dot(before, onehot.astype(BF16)) + base
        rank = jnp.sum(onehot * earlier, axis=-1, keepdims=True).astype(jnp.int32)
        base = base + jnp.sum(onehot, axis=0, keepdims=True)
        r_out = jnp.zeros((rb, LANES), jnp.int32)
        for k in range(TOP_K):
            r_out = jnp.where(wide_rb == k, rank[k * rb:(k + 1) * rb], r_out)
        ranks.append(r_out)
    rank_ref[...] = jnp.concatenate(ranks, axis=0)
    base_scr[...] = jnp.broadcast_to(base, base_scr.shape)
    cnt_ref[...] = jnp.broadcast_to(base, cnt_ref.shape)


def _norm_router(x, g, mod3, wr, br, geom, l):
    n, d = x.shape
    tm = geom.tm
    n_exp = wr.shape[1]
    return pl.pallas_call(
        functools.partial(_norm_router_body, geom=geom, n_exp=n_exp),
        grid=(n // tm,),
        in_specs=[pl.BlockSpec((tm, d), lambda i: (i, 0)),
                  pl.BlockSpec((1, d), lambda i: (l, 0)),
                  pl.BlockSpec(mod3.shape, lambda i: (0, 0, 0)),
                  pl.BlockSpec((d, n_exp), lambda i: (l, 0)),
                  pl.BlockSpec((1, n_exp), lambda i: (l, 0))],
        out_specs=[pl.BlockSpec((tm, d), lambda i: (i, 0)),
                   pl.BlockSpec((tm, LANES), lambda i: (i, 0)),
                   pl.BlockSpec((tm, LANES), lambda i: (i, 0)),
                   pl.BlockSpec((tm, LANES), lambda i: (i, 0)),
                   pl.BlockSpec((8, LANES), lambda i: (0, 0))],
        out_shape=[jax.ShapeDtypeStruct((n, d), F32),
                   jax.ShapeDtypeStruct((n, LANES), jnp.int32),
                   jax.ShapeDtypeStruct((n, LANES), F32),
                   jax.ShapeDtypeStruct((n, LANES), jnp.int32),
                   jax.ShapeDtypeStruct((8, LANES), F32)],
        scratch_shapes=[pltpu.VMEM((8, LANES), F32)],
        compiler_params=_cp(("arbitrary",)),
        name="norm_router",
    )(x, g, mod3, wr, br)


def _row_copy(src, dst, sem, s_row, d_slot):
    return pltpu.make_async_copy(src.at[pl.ds(s_row, 1), :], dst.at[pl.ds(d_slot, 1), :], sem)


DMA_UNROLL = 8


def _row_gather_step(idx_ref, src_ref, buf, sem, n_rows, n_steps, live_ref=None):
    i = pl.program_id(0)

    def groups(step):
        return (n_rows if live_ref is None else live_ref[step]) // DMA_UNROLL

    def issue(step, slot):
        def start(g, carry):
            for k in range(DMA_UNROLL):
                r = g * DMA_UNROLL + k
                _row_copy(src_ref, buf.at[slot], sem.at[slot], idx_ref[step * n_rows + r], r).start()
            return carry

        lax.fori_loop(0, groups(step), start, 0)

    @pl.when(i == 0)
    def _():
        issue(0, 0)

    @pl.when(i + 1 < n_steps)
    def _():
        issue(i + 1, (i + 1) % 2)

    slot = i % 2

    def wait(g, carry):
        for k in range(DMA_UNROLL):
            _row_copy(src_ref, buf.at[slot], sem.at[slot], 0, g * DMA_UNROLL + k).wait()
        return carry

    lax.fori_loop(0, groups(i), wait, 0)
    return slot


def _gather_body(rows_ref, live_ref, src_ref, o_ref, buf, sem, *, n_rows, n_steps):
    @pl.when(pl.program_id(0) == 0)
    def _():
        buf[...] = jnp.zeros(buf.shape, buf.dtype)

    slot = _row_gather_step(rows_ref, src_ref, buf, sem, n_rows, n_steps, live_ref)
    o_ref[...] = buf[slot].astype(o_ref.dtype)


def _moe_gather(rows, live, h2, tm):
    p_max = rows.shape[0]
    d = h2.shape[1]
    return pl.pallas_call(
        functools.partial(_gather_body, n_rows=tm, n_steps=p_max // tm),
        grid_spec=pltpu.PrefetchScalarGridSpec(
            num_scalar_prefetch=2,
            grid=(p_max // tm,),
            in_specs=[pl.BlockSpec(memory_space=pl.ANY)],
            out_specs=pl.BlockSpec((tm, d), lambda i, rows, live: (i, 0)),
            scratch_shapes=[pltpu.VMEM((2, tm, d), F32), pltpu.SemaphoreType.DMA((2,))]),
        out_shape=jax.ShapeDtypeStruct((p_max, d), BF16),
        compiler_params=_cp(("arbitrary",)),
        name="moe_gather",
    )(rows, live, h2)


def _expert_weights(plan, w_hbm, wbuf, wbf, sem, col_blocks, tn, e0, n_pass):
    blk_e, _, first, grp, e_next, meta, _ = plan
    j = pl.program_id(0)
    i = pl.program_id(1)
    n_grp = meta[1]

    def copies(e_idx, jj):
        return [pltpu.make_async_copy(w_hbm.at[e0 + e_idx, :, pl.ds(cb * tn, tn)], wbuf.at[p], sem.at[p])
                for p, cb in enumerate(col_blocks(jj))]

    @pl.when(first[i] == 1)
    def _():
        g = grp[i]

        @pl.when((j == 0) & (g == 0))
        def _():
            for c in copies(blk_e[0], 0):
                c.start()

        for c in copies(blk_e[i], j):
            c.wait()
        rows_per = 512 if wbf.shape[1] % 512 == 0 else wbf.shape[1]

        def cast_rows(r, carry):
            rows = pl.ds(pl.multiple_of(r * rows_per, rows_per), rows_per)
            for p in range(wbf.shape[0]):
                wbf[p, rows, :] = wbuf[p, rows, :].astype(BF16)
            return carry

        lax.fori_loop(0, wbf.shape[1] // rows_per, cast_rows, 0)
        last_in_pass = g == n_grp - 1

        @pl.when(jnp.logical_not(last_in_pass & (j == n_pass - 1)))
        def _():
            for c in copies(e_next[i], jnp.where(last_in_pass, j + 1, j)):
                c.start()


N_PLAN = 7
TILE_PARTS = 4


def _for_tile_rows(plan, o_ref, compute):
    i = pl.program_id(1)
    nv = plan[5][0]
    quarters = plan[6][i]
    tm = o_ref.shape[0]
    step = tm // TILE_PARTS

    for k in range(1, TILE_PARTS + 1):
        @pl.when((i < nv) & (quarters == k))
        def _(k=k):
            compute(pl.ds(0, k * step))
            if k < TILE_PARTS:
                o_ref[pl.ds(k * step, tm - k * step), :] = jnp.zeros((tm - k * step, o_ref.shape[1]), o_ref.dtype)

    @pl.when(i >= nv)
    def _():
        o_ref[...] = jnp.zeros(o_ref.shape, o_ref.dtype)


def _moe_gu_body(*refs, tn, e0, nj):
    plan, (x_ref, w_hbm, bg_ref, bu_ref, o_ref, wbuf, wbf, sem) = refs[:N_PLAN], refs[N_PLAN:]
    _expert_weights(plan, w_hbm, wbuf, wbf, sem, lambda jj: (jj, nj + jj), tn, e0, nj)

    def compute(rows):
        x = x_ref[rows, :]
        g = jnp.minimum(_dot(x, wbf[0]) + bg_ref[0], SWIGLU_LIMIT)
        u = jnp.clip(_dot(x, wbf[1]) + bu_ref[0], -SWIGLU_LIMIT, SWIGLU_LIMIT)
        o_ref[rows, :] = ((u + 1.0) * g * _sigmoid(SWIGLU_ALPHA * g)).astype(o_ref.dtype)

    _for_tile_rows(plan, o_ref, compute)


def _moe_gate_up(plan, xs, w, b, l, n_exp, tm):
    p_max, d = xs.shape
    de = w.shape[2] // 2
    tn = _pick(de, 512)
    nj = de // tn
    e0 = l * n_exp
    return pl.pallas_call(
        functools.partial(_moe_gu_body, tn=tn, e0=e0, nj=nj),
        grid_spec=pltpu.PrefetchScalarGridSpec(
            num_scalar_prefetch=N_PLAN,
            grid=(nj, p_max // tm),
            in_specs=[pl.BlockSpec((tm, d), lambda j, i, be, rb, *_: (rb[i], 0)),
                      pl.BlockSpec(memory_space=pl.ANY),
                      pl.BlockSpec((1, 1, tn), lambda j, i, be, *_: (e0 + be[i], 0, j)),
                      pl.BlockSpec((1, 1, tn), lambda j, i, be, *_: (e0 + be[i], 0, nj + j))],
            out_specs=pl.BlockSpec((tm, tn), lambda j, i, *_: (i, j)),
            scratch_shapes=[pltpu.VMEM((2, d, tn), F32), pltpu.VMEM((2, d, tn), BF16),
                            pltpu.SemaphoreType.DMA((2,))]),
        out_shape=jax.ShapeDtypeStruct((p_max, de), BF16),
        compiler_params=_cp(("arbitrary", "arbitrary")),
        name="moe_gate_up",
    )(*plan, xs, w, b, b)


def _moe_down_body(*refs, tn, e0, nj):
    plan, (x_ref, w_hbm, b_ref, o_ref, wbuf, wbf, sem) = refs[:N_PLAN], refs[N_PLAN:]
    _expert_weights(plan, w_hbm, wbuf, wbf, sem, lambda jj: (jj,), tn, e0, nj)

    def compute(rows):
        o_ref[rows, :] = _dot(x_ref[rows, :], wbf[0]) + b_ref[0]

    _for_tile_rows(plan, o_ref, compute)


def _moe_down(plan, act, w, b, l, n_exp, tm):
    p_max, de = act.shape
    d = w.shape[2]
    tn = _pick(d, 1024)
    nj = d // tn
    e0 = l * n_exp
    return pl.pallas_call(
        functools.partial(_moe_down_body, tn=tn, e0=e0, nj=nj),
        grid_spec=pltpu.PrefetchScalarGridSpec(
            num_scalar_prefetch=N_PLAN,
            grid=(nj, p_max // tm),
            in_specs=[pl.BlockSpec((tm, de), lambda j, i, be, rb, *_: (rb[i], 0)),
                      pl.BlockSpec(memory_space=pl.ANY),
                      pl.BlockSpec((1, 1, tn), lambda j, i, be, *_: (e0 + be[i], 0, j))],
            out_specs=pl.BlockSpec((tm, tn), lambda j, i, *_: (i, j)),
            scratch_shapes=[pltpu.VMEM((1, de, tn), F32), pltpu.VMEM((1, de, tn), BF16),
                            pltpu.SemaphoreType.DMA((1,))]),
        out_shape=jax.ShapeDtypeStruct((p_max, d), F32),
        compiler_params=_cp(("arbitrary", "arbitrary")),
        name="moe_down",
    )(*plan, act, w, b)


def _combine_body(dest_ref, yb_ref, x_ref, p_ref, mod_ref, op_ref, os_ref, buf, sem, *, geom, n_steps):
    i = pl.program_id(0)
    slot = _row_gather_step(dest_ref, yb_ref, buf, sem, ROW_BLOCK * TOP_K, n_steps)
    b = geom.bidx(i)
    p = p_ref[...]
    acc = p[:, 0:1] * buf[slot, pl.ds(0, ROW_BLOCK), :]
    for k in range(1, TOP_K):
        acc = acc + p[:, k:k + 1] * buf[slot, pl.ds(k * ROW_BLOCK, ROW_BLOCK), :]
    val = x_ref[...] + mod_ref[b, pl.ds(5, 1), :] * acc
    n_pb = geom.n_p // ROW_BLOCK

    @pl.when(i < n_pb)
    def _():
        op_ref[...] = val

    @pl.when(i >= n_pb)
    def _():
        os_ref[...] = val


def _moe_combine(dest_km, yb, x1, probs, mod3, geom):
    n, d = x1.shape
    n_pb = geom.n_p // ROW_BLOCK
    return pl.pallas_call(
        functools.partial(_combine_body, geom=geom, n_steps=n // ROW_BLOCK),
        grid_spec=pltpu.PrefetchScalarGridSpec(
            num_scalar_prefetch=1,
            grid=(n // ROW_BLOCK,),
            in_specs=[pl.BlockSpec(memory_space=pl.ANY),
                      pl.BlockSpec((ROW_BLOCK, d), lambda i, dest: (i, 0)),
                      pl.BlockSpec((ROW_BLOCK, LANES), lambda i, dest: (i, 0)),
                      pl.BlockSpec(mod3.shape, lambda i, dest: (0, 0, 0))],
            out_specs=[pl.BlockSpec((ROW_BLOCK, d), lambda i, dest: (jnp.minimum(i, n_pb - 1), 0)),
                       pl.BlockSpec((ROW_BLOCK, d), lambda i, dest: (jnp.maximum(i - n_pb, 0), 0))],
            scratch_shapes=[pltpu.VMEM((2, ROW_BLOCK * TOP_K, d), F32), pltpu.SemaphoreType.DMA((2,))]),
        out_shape=[jax.ShapeDtypeStruct((geom.n_p, d), F32), jax.ShapeDtypeStruct((geom.n_s, d), F32)],
        compiler_params=_cp(("arbitrary",)),
        name="moe_combine",
    )(dest_km, yb, x1, probs, mod3)


def _moe_plan(top_idx, rank, counts, n_exp, tm, gather_rows):
    n_tok = top_idx.shape[0]
    n_slot = n_tok * TOP_K
    flat_e = top_idx.reshape(n_slot)
    counts = counts.astype(jnp.int32)
    p_counts = (counts + tm - 1) // tm * tm
    p_ends = jnp.cumsum(p_counts)
    p_starts = p_ends - p_counts
    onehot = flat_e[:, None] == jnp.arange(n_exp, dtype=jnp.int32)[None, :]
    dest = (jnp.sum(jnp.where(onehot, p_starts[None, :], 0), axis=1) + rank.reshape(n_slot)).astype(jnp.int32)
    n_tiles = -(-n_slot // tm) + n_exp
    p_max = n_tiles * tm
    flat_tok = jnp.arange(n_slot, dtype=jnp.int32) // TOP_K
    rows = jnp.zeros((p_max,), jnp.int32).at[dest].set(flat_tok, unique_indices=True)
    tile_start = jnp.arange(n_tiles, dtype=jnp.int32) * tm
    nv = (p_ends[-1] // tm).astype(jnp.int32)
    blk_e = jnp.minimum(jnp.searchsorted(p_ends, tile_start, side="right"), n_exp - 1).astype(jnp.int32)
    last_e = blk_e[jnp.maximum(nv - 1, 0)]
    tile_id = jnp.arange(n_tiles, dtype=jnp.int32)
    blk_e = jnp.where(tile_id < nv, blk_e, last_e)
    row_blk = jnp.minimum(tile_id, jnp.maximum(nv - 1, 0))
    first = (tile_id < nv) & ((tile_id == 0) | (blk_e != jnp.roll(blk_e, 1)))
    grp = jnp.cumsum(first.astype(jnp.int32)) - 1
    n_grp = jnp.sum(first.astype(jnp.int32))
    starts = jnp.sort(jnp.where(first, tile_id, n_tiles))
    grp_e = blk_e[jnp.minimum(starts, n_tiles - 1)]
    e_next = grp_e[(grp + 1) % n_grp]
    meta = jnp.stack([nv, n_grp]).astype(jnp.int32)
    live = jnp.minimum(counts[blk_e] - (tile_start - p_starts[blk_e]), tm)
    part = tm // TILE_PARTS
    parts = jnp.where(tile_id < nv, jnp.clip((live + part - 1) // part, 1, TILE_PARTS), 0).astype(jnp.int32)
    plan = (blk_e, row_blk, first.astype(jnp.int32), grp.astype(jnp.int32), e_next.astype(jnp.int32), meta, parts)
    dest_km = dest.reshape(n_tok // ROW_BLOCK, ROW_BLOCK, TOP_K).transpose(0, 2, 1).reshape(n_slot)
    per_tile = tm // gather_rows
    step_id = jnp.arange(n_tiles * per_tile, dtype=jnp.int32)
    tile_live = jnp.where(tile_id < nv, live, 0)
    glive = jnp.clip(tile_live[step_id // per_tile] - (step_id % per_tile) * gather_rows, 0, gather_rows)
    glive = ((glive + DMA_UNROLL - 1) // DMA_UNROLL * DMA_UNROLL).astype(jnp.int32)
    return rows, glive, dest_km, plan


def kernel(x_prompt, x_sample, cache_k, cache_v, state_ssm_re, state_ssm_im, c_prompt, c_sample,
           w_ada, b_ada, g_norm_mix, g_norm_ffn, w_in, b_in, g_q, g_k, w_attn_up,
           ssm_a_re, ssm_a_im, ssm_log_dt, ssm_b_re, ssm_b_im, ssm_c_re, ssm_c_im, ssm_d,
           w_glu, w_out, w_router, b_router, w_gate_up, b_gate_up, w_down, b_down):
    B, T, D = x_prompt.shape
    DB, DT, _ = x_sample.shape
    depth = w_ada.shape[0]
    H, hd = cache_k.shape[3], cache_k.shape[4]
    AW = H * hd
    G, N, P = ssm_b_re.shape[1:]
    SW = G * P
    E = w_router.shape[2]
    assert hd == LANES and P == 16 and 2 * N == LANES and DT == ROW_BLOCK
    geom = _Geom(B, T, DB, DT)
    n_p, n_tok, nbp = geom.n_p, geom.n_tok, geom.nbp
    L = SSM_CHUNK
    assert T % L == 0 and DT % L == 0
    moe_tm = 512
    gather_rows = 256

    xp, xs_ = x_prompt.reshape(n_p, D), x_sample.reshape(geom.n_s, D)
    c_all = jnp.concatenate([c_prompt, c_sample, jnp.zeros((nbp - geom.nb, D), F32)], axis=0)

    flat2 = lambda w: w.reshape((w.shape[0] * w.shape[1],) + w.shape[2:])
    w_ada2, w_in2, w_au2, w_glu2, w_out2, w_r2 = map(flat2, (w_ada, w_in, w_attn_up, w_glu, w_out, w_router))
    w_gu3, w_dn3 = flat2(w_gate_up), flat2(w_down)
    b_gu3 = b_gate_up.reshape(depth * E, 1, b_gate_up.shape[2])
    b_dn3 = b_down.reshape(depth * E, 1, b_down.shape[2])

    NG = LANES // P
    assert G % NG == 0
    GB, NS = G // NG, NG * N
    lam = jnp.stack([ssm_a_re.reshape(depth * GB, NS), ssm_a_im.reshape(depth * GB, NS)], axis=1)
    ldt = jnp.repeat(ssm_log_dt, N, axis=-1).reshape(depth * GB, 1, NS)
    bb = jnp.stack([ssm_b_re.reshape(depth * GB, NS, P), ssm_b_im.reshape(depth * GB, NS, P)], axis=1)
    cb = jnp.stack([ssm_c_re.reshape(depth * GB, LANES, N), ssm_c_im.reshape(depth * GB, LANES, N)], axis=1)
    db = ssm_d.reshape(depth * GB, 1, LANES)

    nc_p, nc_s = n_p // L, geom.n_s // L
    NC = nc_p + nc_s
    cps, css = T // L, DT // L
    lidx = jnp.concatenate([jnp.arange(nc_p, dtype=jnp.int32) % cps,
                            jnp.arange(nc_s, dtype=jnp.int32) % css]).reshape(NC, 1)
    n_steps = max(cps - 1, css - 1, 1).bit_length()
    last_rows = tuple((b + 1) * cps - 1 for b in range(B)) + tuple(nc_p + (b + 1) * css - 1 for b in range(DB))
    first_rows = tuple(nc_p + b * css for b in range(DB))

    ks, vs, srs, sis = [], [], [], []
    for l in range(depth):
        mod = _ada(c_all, w_ada2, b_ada, l)
        mod3 = mod.reshape(nbp, 6, D)
        h = _norm_mod(xp, xs_, g_norm_mix, mod3, geom, l, 0, 1)

        dn = functools.partial(_dense, h, w_in2, l, geom=geom, bias=b_in)
        split_f32 = ((F32, "p"), (F32, "s"), (BF16, "all"))
        (qb,) = dn(0, AW, mode="q", extra=(g_q,), extra_kinds=("row",), out_kinds=((BF16, "all"),), name="proj_q")
        kfp, kfs, kb = dn(AW, AW, mode="k", extra=(g_k,), extra_kinds=("row",), out_kinds=split_f32, name="proj_k")
        vfp, vfs, vb = dn(2 * AW, AW, mode="v", out_kinds=split_f32, name="proj_v")
        (u,) = dn(3 * AW, SW, mode="plain", name="proj_u")
        (sga,) = dn(3 * AW + SW, D, mode="sig", name="proj_ga")
        (sgb,) = dn(3 * AW + SW + D, D, mode="sig", name="proj_gb")

        attn = jnp.concatenate([_sb_prompt(qb, kb, vb, B, T, H, hd),
                                _sb_sample(qb, kb, vb, cache_k, cache_v, l, DB, DT, H, hd, n_p)], axis=0)
        (ma,) = _dense(attn, w_au2, l, 0, D, mode="mul", geom=geom, extra=(sga,), extra_kinds=("tile",),
                       name="attn_up")

        x0 = jnp.concatenate([state_ssm_re[l].reshape(DB, GB, NS), state_ssm_im[l].reshape(DB, GB, NS)],
                             axis=-1).transpose(1, 0, 2)
        yg, fin = _s5(u, x0, lidx, lam, ldt, bb, cb, db, l, N, n_steps, first_rows, last_rows)
        fin = fin.reshape(GB, geom.nb, 2, NG, N).transpose(2, 1, 0, 3, 4).reshape(2, geom.nb, G, N)

        merged = _glu_merge(yg, w_glu2, l, ma, sgb, geom)
        (x1,) = _dense(merged, w_out2, l, 0, D, mode="resid", geom=geom, extra=(xp, xs_, mod3),
                       extra_kinds=("tile_p", "tile_s", "mod"), name="out_proj")

        h2, top_idx, probs, rank, cnt = _norm_router(x1, g_norm_ffn, mod3, w_r2, b_router, geom, l)
        rows, glive, dest_km, plan = _moe_plan(top_idx[:, :TOP_K], rank[:, :TOP_K], cnt[0, :E], E, moe_tm,
                                               gather_rows)
        xs = _moe_gather(rows, glive, h2, gather_rows)
        act = _moe_gate_up(plan, xs, w_gu3, b_gu3, l, E, moe_tm)
        yb = _moe_down(plan, act, w_dn3, b_dn3, l, E, moe_tm)
        xp, xs_ = _moe_combine(dest_km, yb, x1, probs, mod3, geom)

        ks.append((kfp, kfs))
        vs.append((vfp, vfs))
        srs.append(fin[0])
        sis.append(fin[1])

    def stacked(pairs, part, lead):
        return jnp.stack([p[part] for p in pairs]).reshape((depth,) + lead + (H, hd))

    sr, si = jnp.stack(srs), jnp.stack(sis)
    return (xp.reshape(B, T, D), xs_.reshape(DB, DT, D), stacked(ks, 0, (B, T)), stacked(vs, 0, (B, T)),
            sr[:, :B], si[:, :B], stacked(ks, 1, (DB, DT)), stacked(vs, 1, (DB, DT)), sr[:, B:], si[:, B:])
```
